```python
import jax, jax.numpy as jnp
from jax import lax
import numpy as np

D_MODEL = 2048
BATCH = 16
SEQ = 2048
DEPTH = 2

CHUNK = 64
Q_BLOCK = 128
D_MIX = D_MODEL
HEAD_DIM = 128
D_CONV = D_MIX // 4
CONV_WIDTH = 31
FOX_HEADS = 6
DSA_HEADS = 6
D_FOX = FOX_HEADS * HEAD_DIM
D_DSA = DSA_HEADS * HEAD_DIM
IDX_HEADS = 16
IDX_DIM = 64
TOPK_MAX = 256
ROPE_THETA = 10000.0
N_GROUPS = 4
EXPERTS_PER_GROUP = 8
N_EXPERTS = N_GROUPS * EXPERTS_PER_GROUP
TOP_EXPERTS = 2
D_EXPERT = 512
MOE_BLOCK = 256
EPS = 1e-6
IN_SPLITS = (2 * D_CONV, D_FOX, D_FOX, D_FOX, FOX_HEADS, D_DSA, HEAD_DIM, HEAD_DIM, IDX_HEADS * IDX_DIM, IDX_DIM, IDX_HEADS)
D_IN = 2 * D_CONV + 3 * D_FOX + FOX_HEADS + D_DSA + 2 * HEAD_DIM + IDX_HEADS * IDX_DIM + IDX_DIM + IDX_HEADS

kernel_name = "hybrid_conv_fox_dsa_hiermoe"


def rmsnorm(x, g):
    xf = x.astype(jnp.float32)
    y = xf * lax.rsqrt(jnp.mean(xf * xf, axis=-1, keepdims=True) + EPS)
    return (y * g.astype(jnp.float32)).astype(x.dtype)


def layernorm(x, g, b):
    xf = x.astype(jnp.float32)
    mu = jnp.mean(xf, axis=-1, keepdims=True)
    var = jnp.mean(jnp.square(xf - mu), axis=-1, keepdims=True)
    y = (xf - mu) * lax.rsqrt(var + EPS)
    return (y * g.astype(jnp.float32) + b.astype(jnp.float32)).astype(x.dtype)


def rope_tables(L, dim):
    pos = jnp.arange(L, dtype=jnp.float32)
    inv = ROPE_THETA ** (-jnp.arange(0, dim, 2, dtype=jnp.float32) / dim)
    ang = pos[:, None] * inv[None, :]
    return jnp.cos(ang), jnp.sin(ang)


def apply_rope(x, cos, sin):
    extra = x.ndim - 3
    c = cos.reshape(cos.shape[0], *([1] * extra), cos.shape[1])
    s = sin.reshape(sin.shape[0], *([1] * extra), sin.shape[1])
    xf = x.astype(jnp.float32)
    x1, x2 = jnp.split(xf, 2, axis=-1)
    return jnp.concatenate([x1 * c - x2 * s, x2 * c + x1 * s], axis=-1).astype(x.dtype)


def conformer_conv(u, conv_w, conv_b, ln_g, ln_b):
    a = u[..., :D_CONV] * jax.nn.sigmoid(u[..., D_CONV:])
    a = lax.conv_general_dilated(a, conv_w[:, None, :], window_strides=(1,), padding=[(CONV_WIDTH - 1, 0)],
                                 dimension_numbers=("NWC", "WIO", "NWC"), feature_group_count=D_CONV) + conv_b
    return jax.nn.silu(layernorm(a, ln_g, ln_b))


def to_blocks(t, nb):
    return t.reshape(t.shape[0], nb, Q_BLOCK, *t.shape[2:]).swapaxes(0, 1)


def fox_attention(q, k, v, c, pos):
    B, L, H, Dh = q.shape
    nb = L // Q_BLOCK
    cT = c.swapaxes(1, 2)

    def one(args):
        q_b, c_b, pq = args
        s = jnp.einsum("bqhd,bkhd->bhqk", q_b, k).astype(jnp.float32) * (Dh ** -0.5)
        s = s + c_b.swapaxes(1, 2)[..., :, None] - cT[..., None, :]
        s = jnp.where((pq[:, None] >= pos[None, :])[None, None], s, -jnp.inf)
        p = jax.nn.softmax(s, axis=-1).astype(v.dtype)
        return jnp.einsum("bhqk,bkhd->bqhd", p, v)

    out = lax.map(one, (to_blocks(q, nb), to_blocks(c, nb), pos.reshape(nb, Q_BLOCK)))
    return out.swapaxes(0, 1).reshape(B, L, H * Dh)


def dsa_attention(q, k, v, qi, ki, wi, chunk_id, k_sel):
    B, L, H, Dh = q.shape
    nb = L // Q_BLOCK
    bidx = jnp.arange(B)[:, None, None]

    def one(args):
        q_b, qi_b, wi_b, cq = args
        rel = jax.nn.relu(jnp.einsum("bqhd,bkd->bqhk", qi_b, ki).astype(jnp.float32) * (IDX_DIM ** -0.5))
        score = jnp.einsum("bqhk,bqh->bqk", rel, wi_b.astype(jnp.float32)) * (IDX_HEADS ** -0.5)
        admissible = chunk_id[None, :] <= cq[:, None]
        score = jnp.where(admissible[None], score, -jnp.inf)
        top_s, top_i = lax.top_k(score, k_sel)
        valid = jnp.isfinite(top_s)
        ks = k[bidx, top_i]
        vs = v[bidx, top_i]
        logit = jnp.einsum("bqhd,bqkd->bqhk", q_b, ks).astype(jnp.float32) * (Dh ** -0.5)
        logit = jnp.where(valid[:, :, None, :], logit, -jnp.inf)
        p = jax.nn.softmax(logit, axis=-1).astype(v.dtype)
        return jnp.einsum("bqhk,bqkd->bqhd", p, vs)

    out = lax.map(one, (to_blocks(q, nb), to_blocks(qi, nb), to_blocks(wi, nb), chunk_id.reshape(nb, Q_BLOCK)))
    return out.swapaxes(0, 1).reshape(B, L, H * Dh)


def hier_moe(h, w_rg, b_rg, w_re, b_re, w_gate, w_up, w_down):
    B, L, D = h.shape
    T = B * L
    xt = h.reshape(T, D)
    g_prob = jax.nn.softmax((xt @ w_rg + b_rg).astype(jnp.float32), axis=-1)
    p_g, g_idx = lax.top_k(g_prob, 1)
    e_logits = (xt @ w_re + b_re).astype(jnp.float32).reshape(T, N_GROUPS, EXPERTS_PER_GROUP)
    e_in = jnp.take_along_axis(e_logits, g_idx[:, :, None], axis=1)[:, 0]
    e_prob = jax.nn.softmax(e_in, axis=-1)
    p_e, e_idx = lax.top_k(e_prob, TOP_EXPERTS)
    gate = p_g * p_e / jnp.sum(p_e, axis=-1, keepdims=True)
    expert_id = g_idx * EXPERTS_PER_GROUP + e_idx

    A = T * TOP_EXPERTS
    nblk = -(-A // MOE_BLOCK) + N_EXPERTS
    npad = nblk * MOE_BLOCK
    eid = expert_id.reshape(A)
    tok = jnp.repeat(jnp.arange(T, dtype=jnp.int32), TOP_EXPERTS)
    wts = gate.reshape(A)
    order = jnp.argsort(eid)
    s_e, s_tok, s_w = eid[order], tok[order], wts[order]
    counts = jnp.bincount(eid, length=N_EXPERTS)
    starts = jnp.cumsum(counts) - counts
    padded = (counts + MOE_BLOCK - 1) // MOE_BLOCK * MOE_BLOCK
    pends = jnp.cumsum(padded)
    pstarts = pends - padded
    dest = pstarts[s_e] + (jnp.arange(A) - starts[s_e])
    buf_tok = jnp.zeros((npad,), jnp.int32).at[dest].set(s_tok)
    buf_w = jnp.zeros((npad,), xt.dtype).at[dest].set(s_w.astype(xt.dtype))
    blk_e = jnp.clip(jnp.searchsorted(pends, jnp.arange(nblk) * MOE_BLOCK, side="right"), 0, N_EXPERTS - 1)

    def run(args):
        tok_b, w_b, e = args
        xb = xt[tok_b]
        hdn = jax.nn.silu(xb @ w_gate[e]) * (xb @ w_up[e])
        return (hdn @ w_down[e]) * w_b[:, None]

    yb = lax.map(run, (buf_tok.reshape(nblk, MOE_BLOCK), buf_w.reshape(nblk, MOE_BLOCK), blk_e))
    out = jnp.zeros((T, D), xt.dtype).at[buf_tok].add(yb.reshape(npad, D))
    return out.reshape(B, L, D)


def hybrid_mixer(h, w_in, conv_w, conv_b, ln_g, ln_b, fox_fb, w_out, rope_h, rope_i, pos, chunk_id, k_sel):
    B, L, _ = h.shape
    z = h @ w_in
    split_at = np.cumsum(IN_SPLITS)[:-1].tolist()
    u, qf, kf, vf, fl, qd, kd, vd, qi, ki, wi = jnp.split(z, split_at, axis=-1)
    y_a = conformer_conv(u, conv_w, conv_b, ln_g, ln_b)
    log_f = jax.nn.log_sigmoid(fl.astype(jnp.float32) + fox_fb.astype(jnp.float32))
    c = jnp.cumsum(log_f, axis=1)
    y_b = fox_attention(qf.reshape(B, L, FOX_HEADS, HEAD_DIM), kf.reshape(B, L, FOX_HEADS, HEAD_DIM),
                        vf.reshape(B, L, FOX_HEADS, HEAD_DIM), c, pos)
    cos_h, sin_h = rope_h
    cos_i, sin_i = rope_i
    y_c = dsa_attention(apply_rope(qd.reshape(B, L, DSA_HEADS, HEAD_DIM), cos_h, sin_h),
                        apply_rope(kd, cos_h, sin_h), vd,
                        apply_rope(qi.reshape(B, L, IDX_HEADS, IDX_DIM), cos_i, sin_i),
                        apply_rope(ki, cos_i, sin_i), wi, chunk_id, k_sel)
    return jnp.concatenate([y_a, y_b, y_c], axis=-1) @ w_out


def setup_inputs(seed: int = 0) -> dict:
    key = jax.random.key(seed)
    ks = jax.random.split(key, 20)
    f32 = jnp.float32

    def nrm(k, shape, scale):
        return jax.random.normal(k, shape, f32) * scale

    return {
        "x": nrm(ks[0], (BATCH, SEQ, D_MODEL), 1.0),
        "norm_mix": 1.0 + nrm(ks[1], (DEPTH, D_MODEL), 0.02),
        "w_in": nrm(ks[2], (DEPTH, D_MODEL, D_IN), D_MODEL ** -0.5),
        "conv_w": nrm(ks[3], (DEPTH, CONV_WIDTH, D_CONV), CONV_WIDTH ** -0.5),
        "conv_b": nrm(ks[4], (DEPTH, D_CONV), 0.02),
        "conv_ln_g": 1.0 + nrm(ks[5], (DEPTH, D_CONV), 0.02),
        "conv_ln_b": nrm(ks[6], (DEPTH, D_CONV), 0.02),
        "fox_fb": 2.0 + nrm(ks[7], (DEPTH, FOX_HEADS), 0.5),
        "w_out": nrm(ks[8], (DEPTH, D_MIX, D_MODEL), D_MIX ** -0.5),
        "norm_ffn": 1.0 + nrm(ks[9], (DEPTH, D_MODEL), 0.02),
        "w_router_group": nrm(ks[10], (DEPTH, D_MODEL, N_GROUPS), D_MODEL ** -0.5),
        "b_router_group": nrm(ks[11], (DEPTH, N_GROUPS), 0.01),
        "w_router_expert": nrm(ks[12], (DEPTH, D_MODEL, N_EXPERTS), D_MODEL ** -0.5),
        "b_router_expert": nrm(ks[13], (DEPTH, N_EXPERTS), 0.01),
        "w_gate": nrm(ks[14], (DEPTH, N_EXPERTS, D_MODEL, D_EXPERT), D_MODEL ** -0.5),
        "w_up": nrm(ks[15], (DEPTH, N_EXPERTS, D_MODEL, D_EXPERT), D_MODEL ** -0.5),
        "w_down": nrm(ks[16], (DEPTH, N_EXPERTS, D_EXPERT, D_MODEL), D_EXPERT ** -0.5),
        "norm_final": 1.0 + nrm(ks[17], (D_MODEL,), 0.02),
    }


def reference(x, norm_mix, w_in, conv_w, conv_b, conv_ln_g, conv_ln_b, fox_fb, w_out, norm_ffn,
              w_router_group, b_router_group, w_router_expert, b_router_expert, w_gate, w_up, w_down, norm_final):
    L = x.shape[1]
    pos = jnp.arange(L, dtype=jnp.int32)
    chunk_id = pos // CHUNK
    k_sel = min(TOPK_MAX, L // 4)
    rope_h = rope_tables(L, HEAD_DIM)
    rope_i = rope_tables(L, IDX_DIM)
    for l in range(DEPTH):
        h = rmsnorm(x, norm_mix[l])
        x = x + hybrid_mixer(h, w_in[l], conv_w[l], conv_b[l], conv_ln_g[l], conv_ln_b[l], fox_fb[l], w_out[l],
                             rope_h, rope_i, pos, chunk_id, k_sel)
        h = rmsnorm(x, norm_ffn[l])
        x = x + hier_moe(h, w_router_group[l], b_router_group[l], w_router_expert[l], b_router_expert[l],
                         w_gate[l], w_up[l], w_down[l])
    return rmsnorm(x, norm_final)
```

```python
import functools

import jax
import jax.numpy as jnp
import numpy as np
from jax import lax
from jax.experimental import pallas as pl
from jax.experimental.pallas import tpu as pltpu

F32 = jnp.float32
BF16 = jnp.bfloat16
I32 = jnp.int32

D_MODEL = 2048
HEAD_DIM = 128
D_CONV = 512
CONV_WIDTH = 31
FOX_HEADS = 6
DSA_HEADS = 6
D_FOX = FOX_HEADS * HEAD_DIM
D_DSA = DSA_HEADS * HEAD_DIM
IDX_HEADS = 16
IDX_DIM = 64
TOPK_MAX = 256
ROPE_THETA = 10000.0
CHUNK = 64
N_GROUPS = 4
EXPERTS_PER_GROUP = 8
N_EXPERTS = N_GROUPS * EXPERTS_PER_GROUP
D_EXPERT = 512
EPS = 1e-6

LANES = 128
NEG = -1e30
INT_MIN = -(2 ** 31)
VMEM_LIMIT = 56 * 1024 * 1024

_O_U = 0
_O_QF = _O_U + 2 * D_CONV
_O_KF = _O_QF + D_FOX
_O_VF = _O_KF + D_FOX
_O_FL = _O_VF + D_FOX
_O_QD = _O_FL + FOX_HEADS
_O_KD = _O_QD + D_DSA
_O_VD = _O_KD + HEAD_DIM
_O_QI = _O_VD + HEAD_DIM
_O_KI = _O_QI + IDX_HEADS * IDX_DIM
_O_WI = _O_KI + IDX_DIM
_O_END = _O_WI + IDX_HEADS

_S_KI = 0
_S_WI = IDX_DIM
_S_FL = IDX_DIM + IDX_HEADS


def _cparams(sem):
    return pltpu.CompilerParams(dimension_semantics=sem, vmem_limit_bytes=VMEM_LIMIT)


def _proj_body(x_ref, g_ref, w_ref, tab_ref, *rest, segs, has_t):
    if has_t:
        wt_ref, outs = rest[0], rest[1:]
    else:
        wt_ref, outs = None, rest
    x = x_ref[...]
    inv = lax.rsqrt(jnp.mean(x * x, axis=-1, keepdims=True) + EPS)
    h = (x * inv * g_ref[...]).astype(BF16)
    for (kind, w0, width, oi, o0, tabs, scale) in segs:
        z = jnp.dot(h, w_ref[:, w0:w0 + width], preferred_element_type=F32)
        for c in range(width // LANES):
            zc = z[:, c * LANES:(c + 1) * LANES]
            if kind == "plain":
                if scale != 1.0:
                    zc = zc * scale
            elif kind == "rope128":
                zc = zc * tab_ref[tabs[0]] + pltpu.roll(zc, 64, 1) * tab_ref[tabs[1]]
            else:
                zc = (zc * tab_ref[tabs[0]] + pltpu.roll(zc, 96, 1) * tab_ref[tabs[1]]
                      + pltpu.roll(zc, 32, 1) * tab_ref[tabs[2]])
            outs[oi][:, o0 + c * LANES:o0 + (c + 1) * LANES] = zc.astype(outs[oi].dtype)
    if has_t:
        outs[-1][...] = lax.dot_general(wt_ref[...], h, (((1,), (1,)), ((), ())),
                                        preferred_element_type=F32)


def _proj(x2d, g, w, tabs, segs, out_defs, seq_len, wt=None, tm=512):
    T, D = x2d.shape
    tm = min(tm, seq_len)
    nt = T // tm
    nl = seq_len // tm
    ntab = tabs.shape[0]
    in_specs = [
        pl.BlockSpec((tm, D), lambda i: (i, 0)),
        pl.BlockSpec((1, D), lambda i: (0, 0)),
        pl.BlockSpec(w.shape, lambda i: (0, 0)),
        pl.BlockSpec((ntab, tm, LANES), lambda i: (0, i % nl, 0)),
    ]
    args = [x2d, g.reshape(1, D), w, tabs]
    out_shape = [jax.ShapeDtypeStruct((T, n), dt) for (n, dt) in out_defs]
    out_specs = [pl.BlockSpec((tm, n), lambda i: (i, 0)) for (n, dt) in out_defs]
    if wt is not None:
        in_specs.append(pl.BlockSpec(wt.shape, lambda i: (0, 0)))
        args.append(wt)
        out_shape.append(jax.ShapeDtypeStruct((wt.shape[0], T), F32))
        out_specs.append(pl.BlockSpec((wt.shape[0], tm), lambda i: (0, i)))
    return pl.pallas_call(
        functools.partial(_proj_body, segs=tuple(segs), has_t=wt is not None),
        grid=(nt,),
        in_specs=in_specs,
        out_specs=out_specs,
        out_shape=out_shape,
        compiler_params=_cparams(("parallel",)),
        name="proj",
    )(*args)


def _cumsum_body(fl_ref, fb_ref, o_ref):
    z = fl_ref[...] + fb_ref[...]
    x = jnp.minimum(z, 0.0) - jnp.log(1.0 + jnp.exp(-jnp.abs(z)))
    L = x.shape[1]
    lane = lax.broadcasted_iota(I32, x.shape, 1)
    s = 1
    while s < L:
        x = x + jnp.where(lane >= s, pltpu.roll(x, s, 1), 0.0)
        s *= 2
    o_ref[0] = x


def _fox_cumsum(flT, fb8, B, L):
    return pl.pallas_call(
        _cumsum_body,
        grid=(B,),
        in_specs=[pl.BlockSpec((8, L), lambda b: (0, b)),
                  pl.BlockSpec((8, 1), lambda b: (0, 0))],
        out_specs=pl.BlockSpec((1, 8, L), lambda b: (b, 0, 0)),
        out_shape=jax.ShapeDtypeStruct((B, 8, L), F32),
        compiler_params=_cparams(("parallel",)),
        name="fox_cumsum",
    )(flT, fb8)


_CONV_HALO = 32


def _conv_body(u_ref, w_ref, cb_ref, g_ref, b_ref, o_ref, a_scr, *, rc):
    L = u_ref.shape[1]
    a_scr[0:_CONV_HALO, :] = jnp.zeros((_CONV_HALO, D_CONV), F32)
    u1 = u_ref[0, :, 0:D_CONV].astype(F32)
    u2 = u_ref[0, :, D_CONV:2 * D_CONV].astype(F32)
    a_scr[_CONV_HALO:_CONV_HALO + L, :] = u1 * jax.nn.sigmoid(u2)
    win = rc + _CONV_HALO

    def chunk(r, carry):
        base = pl.multiple_of(r * rc, 8)
        wnd = a_scr[pl.ds(base, win), :]
        acc = jnp.zeros((rc, D_CONV), F32) + cb_ref[...]
        for j in range(CONV_WIDTH):
            off = _CONV_HALO - (CONV_WIDTH - 1) + j
            sh = pltpu.roll(wnd, win - off, 0)[0:rc]
            acc = acc + sh * w_ref[j:j + 1, :]
        mu = jnp.mean(acc, axis=-1, keepdims=True)
        d = acc - mu
        var = jnp.mean(d * d, axis=-1, keepdims=True)
        y = d * lax.rsqrt(var + EPS) * g_ref[...] + b_ref[...]
        o_ref[0, pl.ds(base, rc), :] = (y * jax.nn.sigmoid(y)).astype(o_ref.dtype)
        return carry

    lax.fori_loop(0, L // rc, chunk, 0)


def _conv_module(u, conv_w, conv_b, ln_g, ln_b, rc=256):
    B, L, _ = u.shape
    rc = min(rc, L)
    return pl.pallas_call(
        functools.partial(_conv_body, rc=rc),
        grid=(B,),
        in_specs=[pl.BlockSpec((1, L, 2 * D_CONV), lambda b: (b, 0, 0)),
                  pl.BlockSpec((CONV_WIDTH, D_CONV), lambda b: (0, 0)),
                  pl.BlockSpec((1, D_CONV), lambda b: (0, 0)),
                  pl.BlockSpec((1, D_CONV), lambda b: (0, 0)),
                  pl.BlockSpec((1, D_CONV), lambda b: (0, 0))],
        out_specs=pl.BlockSpec((1, L, D_CONV), lambda b: (b, 0, 0)),
        out_shape=jax.ShapeDtypeStruct((B, L, D_CONV), BF16),
        scratch_shapes=[pltpu.VMEM((L + _CONV_HALO, D_CONV), F32)],
        compiler_params=_cparams(("parallel",)),
        name="conv_module",
    )(u, conv_w, conv_b.reshape(1, -1), ln_g.reshape(1, -1), ln_b.reshape(1, -1))


def _fox_body(q_ref, k_ref, v_ref, c_ref, o_ref, *, tq):
    qi = pl.program_id(2)
    q = q_ref[0]

    def tile(j, carry, diag):
        m, l, acc = carry
        off = pl.multiple_of(j * tq, tq)
        k = k_ref[0, pl.ds(off, tq), :]
        v = v_ref[0, pl.ds(off, tq), :]
        s = lax.dot_general(q, k, (((1,), (1,)), ((), ())), preferred_element_type=F32)
        s = s - c_ref[0, 0, pl.ds(j, 1), :]
        if diag:
            row = lax.broadcasted_iota(I32, s.shape, 0)
            col = lax.broadcasted_iota(I32, s.shape, 1)
            s = jnp.where(row >= col, s, NEG)
        m_new = jnp.maximum(m, jnp.max(s, axis=-1, keepdims=True))
        alpha = jnp.exp(m - m_new)
        p = jnp.exp(s - m_new)
        l = l * alpha + jnp.sum(p, axis=-1, keepdims=True)
        acc = acc * alpha + jnp.dot(p.astype(BF16), v, preferred_element_type=F32)
        return m_new, l, acc

    init = (jnp.full((tq, 1), NEG, F32), jnp.zeros((tq, 1), F32), jnp.zeros((tq, HEAD_DIM), F32))
    carry = lax.fori_loop(0, qi, functools.partial(tile, diag=False), init)
    m, l, acc = tile(qi, carry, True)
    o_ref[0] = (acc / l).astype(o_ref.dtype)


def _fox_attention(fox, c4, tq=256):
    B, L, _ = fox.shape
    tq = min(tq, L)
    nq = L // tq
    H = FOX_HEADS
    return pl.pallas_call(
        functools.partial(_fox_body, tq=tq),
        grid=(B, H, nq),
        in_specs=[pl.BlockSpec((1, tq, HEAD_DIM), lambda b, h, i: (b, i, h)),
                  pl.BlockSpec((1, L, HEAD_DIM), lambda b, h, i: (b, 0, H + h)),
                  pl.BlockSpec((1, L, HEAD_DIM), lambda b, h, i: (b, 0, 2 * H + h)),
                  pl.BlockSpec((1, 1, nq, tq), lambda b, h, i: (b, h, 0, 0))],
        out_specs=pl.BlockSpec((1, tq, HEAD_DIM), lambda b, h, i: (b, i, h)),
        out_shape=jax.ShapeDtypeStruct((B, L, D_FOX), BF16),
        compiler_params=_cparams(("parallel", "parallel", "arbitrary")),
        name="fox_attention",
    )(fox, fox, fox, c4)


def _dsa_body(qd_ref, qi_ref, sq_ref, k_ref, v_ref, sk_ref, o_ref, key_scr, *, tq, k_sel):
    qb = pl.program_id(1)
    L = k_ref.shape[1]
    H = DSA_HEADS
    wq = sq_ref[0, :, _S_WI:_S_WI + IDX_HEADS] * ((IDX_DIM ** -0.5) * (IDX_HEADS ** -0.5))
    qi = qi_ref[0]

    key_scr[...] = jnp.full(key_scr.shape, INT_MIN, I32)
    row = lax.broadcasted_iota(I32, (tq, tq), 0)
    col = lax.broadcasted_iota(I32, (tq, tq), 1)
    admissible_diag = (col // CHUNK) <= (row // CHUNK)

    def score_tile(j, carry):
        off = pl.multiple_of(j * tq, tq)
        ki = sk_ref[0, pl.ds(off, tq), _S_KI:_S_KI + IDX_DIM].astype(BF16)
        s = jnp.zeros((tq, tq), F32)
        for h in range(IDX_HEADS):
            r = lax.dot_general(qi[:, h * IDX_DIM:(h + 1) * IDX_DIM], ki,
                                (((1,), (1,)), ((), ())), preferred_element_type=F32)
            s = s + jnp.maximum(r, 0.0) * wq[:, h:h + 1]
        bits = pltpu.bitcast(s, I32)
        key = jnp.where(bits < 0, bits ^ jnp.int32(0x7FFFFFFF), bits)
        key = jnp.where(jnp.logical_or(j < qb, admissible_diag), key, INT_MIN)
        key_scr[j] = key
        return carry

    lax.fori_loop(0, qb + 1, score_tile, 0)

    keys = key_scr[...]

    def bit_step(i, t):
        cand = t + (jnp.int32(1) << (31 - i))
        cnt = jnp.sum(jnp.where(keys >= cand[None], 1.0, 0.0), axis=-1, keepdims=True)
        cnt = jnp.sum(cnt, axis=0)
        return jnp.where(cnt >= float(k_sel), cand, t)

    thr = lax.fori_loop(0, 32, bit_step, jnp.full((tq, 1), INT_MIN, I32))
    thr = jnp.maximum(thr, INT_MIN + 1)

    qs = jnp.concatenate([qd_ref[0, :, h * HEAD_DIM:(h + 1) * HEAD_DIM] for h in range(H)], axis=0)

    def attn_tile(j, carry):
        m, l, acc = carry
        off = pl.multiple_of(j * tq, tq)
        k = k_ref[0, pl.ds(off, tq), :]
        v = v_ref[0, pl.ds(off, tq), :]
        s = lax.dot_general(qs, k, (((1,), (1,)), ((), ())), preferred_element_type=F32)
        sel = key_scr[j] >= thr
        s = jnp.where(sel[None], s.reshape(H, tq, tq), NEG).reshape(H * tq, tq)
        m_new = jnp.maximum(m, jnp.max(s, axis=-1, keepdims=True))
        alpha = jnp.exp(m - m_new)
        p = jnp.exp(s - m_new)
        l = l * alpha + jnp.sum(p, axis=-1, keepdims=True)
        acc = acc * alpha + jnp.dot(p.astype(BF16), v, preferred_element_type=F32)
        return m_new, l, acc

    init = (jnp.full((H * tq, 1), NEG, F32), jnp.zeros((H * tq, 1), F32),
            jnp.zeros((H * tq, HEAD_DIM), F32))
    m, l, acc = lax.fori_loop(0, qb + 1, attn_tile, init)
    out = acc / l
    for h in range(H):
        o_ref[0, :, h * HEAD_DIM:(h + 1) * HEAD_DIM] = out[h * tq:(h + 1) * tq].astype(o_ref.dtype)


def _dsa_attention(qd, kv, qi, small, k_sel, tq=128):
    B, L, _ = qd.shape
    nq = L // tq
    return pl.pallas_call(
        functools.partial(_dsa_body, tq=tq, k_sel=k_sel),
        grid=(B, nq),
        in_specs=[pl.BlockSpec((1, tq, D_DSA), lambda b, i: (b, i, 0)),
                  pl.BlockSpec((1, tq, IDX_HEADS * IDX_DIM), lambda b, i: (b, i, 0)),
                  pl.BlockSpec((1, tq, LANES), lambda b, i: (b, i, 0)),
                  pl.BlockSpec((1, L, HEAD_DIM), lambda b, i: (b, 0, 0)),
                  pl.BlockSpec((1, L, HEAD_DIM), lambda b, i: (b, 0, 1)),
                  pl.BlockSpec((1, L, LANES), lambda b, i: (b, 0, 0))],
        out_specs=pl.BlockSpec((1, tq, D_DSA), lambda b, i: (b, i, 0)),
        out_shape=jax.ShapeDtypeStruct((B, L, D_DSA), BF16),
        scratch_shapes=[pltpu.VMEM((nq, tq, tq), I32)],
        compiler_params=_cparams(("parallel", "arbitrary")),
        name="dsa_attention",
    )(qd, qi, small, kv, kv, small)


def _outproj_body(x_ref, ya_ref, yb_ref, yc_ref, w_ref, g_ref, wrh_ref, wrl_ref, br_ref,
                  x2_ref, lg_ref):
    acc = x_ref[...]
    acc = acc + jnp.dot(ya_ref[...], w_ref[0:D_CONV, :], preferred_element_type=F32)
    acc = acc + jnp.dot(yb_ref[...], w_ref[D_CONV:D_CONV + D_FOX, :], preferred_element_type=F32)
    acc = acc + jnp.dot(yc_ref[...], w_ref[D_CONV + D_FOX:, :], preferred_element_type=F32)
    x2_ref[...] = acc
    h = acc * lax.rsqrt(jnp.mean(acc * acc, axis=-1, keepdims=True) + EPS) * g_ref[...]
    hh = h.astype(BF16)
    hl = (h - hh.astype(F32)).astype(BF16)
    lg = jnp.dot(hh, wrh_ref[...], preferred_element_type=F32)
    lg = lg + jnp.dot(hh, wrl_ref[...], preferred_element_type=F32)
    lg = lg + jnp.dot(hl, wrh_ref[...], preferred_element_type=F32)
    lg_ref[...] = lg + br_ref[...]


def _outproj(x2d, ya, yb, yc, w_out, g_ffn, wr_hi, wr_lo, br, tm=512):
    T, D = x2d.shape
    tm = min(tm, T)
    row = lambda n: pl.BlockSpec((tm, n), lambda i: (i, 0))
    const = lambda a: pl.BlockSpec(a.shape, lambda i: (0, 0))
    return pl.pallas_call(
        _outproj_body,
        grid=(T // tm,),
        in_specs=[row(D), row(D_CONV), row(D_FOX), row(D_DSA), const(w_out), const(g_ffn),
                  const(wr_hi), const(wr_lo), const(br)],
        out_specs=[row(D), row(LANES)],
        out_shape=[jax.ShapeDtypeStruct((T, D), F32), jax.ShapeDtypeStruct((T, LANES), F32)],
        compiler_params=_cparams(("parallel",)),
        name="outproj",
    )(x2d, ya, yb, yc, w_out, g_ffn, wr_hi, wr_lo, br)


def _route_body(lg_ref, info_ref, gate_ref, cnt_ref):
    i = pl.program_id(0)
    lg = lg_ref[...]
    tm = lg.shape[0]
    lane = lax.broadcasted_iota(I32, lg.shape, 1)
    ninf = -jnp.inf

    lane_f = lane.astype(F32)

    def first_lane(mask):
        return jnp.min(jnp.where(mask, lane_f, float(LANES)), axis=-1, keepdims=True).astype(I32)

    gl = jnp.where(lane < N_GROUPS, lg, ninf)
    gexp = jnp.exp(gl - jnp.max(gl, axis=-1, keepdims=True))
    gprob = gexp / jnp.sum(gexp, axis=-1, keepdims=True)
    p_g = jnp.max(gprob, axis=-1, keepdims=True)
    g_idx = first_lane(gprob == p_g)

    e_lo = N_GROUPS + EXPERTS_PER_GROUP * g_idx
    emask = jnp.logical_and(lane >= e_lo, lane < e_lo + EXPERTS_PER_GROUP)
    el = jnp.where(emask, lg, ninf)
    eexp = jnp.exp(el - jnp.max(el, axis=-1, keepdims=True))
    eprob = jnp.where(emask, eexp / jnp.sum(eexp, axis=-1, keepdims=True), -1.0)
    p1 = jnp.max(eprob, axis=-1, keepdims=True)
    l1 = first_lane(eprob == p1)
    eprob2 = jnp.where(lane == l1, -1.0, eprob)
    p2 = jnp.max(eprob2, axis=-1, keepdims=True)
    l2 = first_lane(eprob2 == p2)
    den = p1 + p2
    gate1 = p_g * p1 / den
    gate2 = p_g * p2 / den
    eid1 = l1 - N_GROUPS
    eid2 = l2 - N_GROUPS

    @pl.when(i == 0)
    def _():
        cnt_ref[...] = jnp.zeros(cnt_ref.shape, F32)

    hot1 = lane == eid1
    hot2 = lane == eid2
    onehot = jnp.where(jnp.logical_or(hot1, hot2), 1.0, 0.0)
    r = lax.broadcasted_iota(I32, (tm, tm), 0)
    c = lax.broadcasted_iota(I32, (tm, tm), 1)
    tri = jnp.where(c < r, 1.0, 0.0).astype(BF16)
    before = jnp.dot(tri, onehot.astype(BF16), preferred_element_type=F32) + cnt_ref[0:1, :]
    rank1 = jnp.sum(jnp.where(hot1, before, 0.0), axis=-1, keepdims=True).astype(I32)
    rank2 = jnp.sum(jnp.where(hot2, before, 0.0), axis=-1, keepdims=True).astype(I32)
    cnt_ref[...] = cnt_ref[...] + jnp.sum(onehot, axis=0, keepdims=True)

    info_ref[...] = jnp.where(lane == 0, eid1, jnp.where(lane == 1, eid2,
                              jnp.where(lane == 2, rank1, jnp.where(lane == 3, rank2, 0))))
    gate_ref[...] = jnp.where(lane == 0, gate1, jnp.where(lane == 1, gate2, 0.0))


def _route(logits, tm=512):
    T = logits.shape[0]
    tm = min(tm, T)
    row = pl.BlockSpec((tm, LANES), lambda i: (i, 0))
    return pl.pallas_call(
        _route_body,
        grid=(T // tm,),
        in_specs=[row],
        out_specs=[row, row, pl.BlockSpec((8, LANES), lambda i: (0, 0))],
        out_shape=[jax.ShapeDtypeStruct((T, LANES), I32), jax.ShapeDtypeStruct((T, LANES), F32),
                   jax.ShapeDtypeStruct((8, LANES), F32)],
        compiler_params=_cparams(("arbitrary",)),
        name="route",
    )(logits)


def _row_copy(src, s, dst, d, sem):
    return pltpu.make_async_copy(src.at[pl.ds(s, 1)], dst.at[pl.ds(d, 1)], sem)


def _dispatch_body(dest_ref, x_hbm, xs_hbm, sem, *, td):
    base = pl.program_id(0) * td

    def issue(t, carry):
        _row_copy(x_hbm, base + t, xs_hbm, dest_ref[0, 0, t], sem).start()
        _row_copy(x_hbm, base + t, xs_hbm, dest_ref[0, 1, t], sem).start()
        return carry

    def drain(t, carry):
        _row_copy(x_hbm, 0, xs_hbm, 0, sem).wait()
        _row_copy(x_hbm, 0, xs_hbm, 0, sem).wait()
        return carry

    lax.fori_loop(0, td, issue, 0)
    lax.fori_loop(0, td, drain, 0)


def _dispatch(x2, dest3, td):
    T, D = x2.shape
    return pl.pallas_call(
        functools.partial(_dispatch_body, td=td),
        grid=(T // td,),
        in_specs=[pl.BlockSpec((1, 2, td), lambda i: (i, 0, 0), memory_space=pltpu.SMEM),
                  pl.BlockSpec(memory_space=pl.ANY)],
        out_specs=pl.BlockSpec(memory_space=pl.ANY),
        out_shape=jax.ShapeDtypeStruct((2 * T, D), F32),
        scratch_shapes=[pltpu.SemaphoreType.DMA(())],
        compiler_params=_cparams(("arbitrary",)),
        name="dispatch",
    )(dest3, x2)


def _expert_body(blk_ref, e_ref, lo_ref, hi_ref, first_ref, xs_ref, g_ref, wg_ref, wu_ref, wd_ref,
                 o_ref):
    t = pl.program_id(0)
    lo = lo_ref[t]
    hi = hi_ref[t]

    @pl.when(first_ref[t] == 1)
    def _():
        o_ref[...] = jnp.zeros(o_ref.shape, F32)

    @pl.when(hi > lo)
    def _():
        x = xs_ref[...]
        h = (x * lax.rsqrt(jnp.mean(x * x, axis=-1, keepdims=True) + EPS) * g_ref[...]).astype(BF16)
        g = jnp.dot(h, wg_ref[0], preferred_element_type=F32)
        u = jnp.dot(h, wu_ref[0], preferred_element_type=F32)
        hdn = (g * jax.nn.sigmoid(g) * u).astype(BF16)
        y = jnp.dot(hdn, wd_ref[0], preferred_element_type=F32)
        row = lax.broadcasted_iota(I32, (y.shape[0], 1), 0)
        keep = jnp.logical_and(row >= lo, row < hi)
        o_ref[...] = o_ref[...] + jnp.where(keep, y, 0.0)


def _expert_ffn(xs, g_ffn, wg, wu, wd, items, bm):
    A, D = xs.shape
    blk, eid, lo, hi, first = items
    ni = blk.shape[0]
    grid_spec = pltpu.PrefetchScalarGridSpec(
        num_scalar_prefetch=5,
        grid=(ni,),
        in_specs=[pl.BlockSpec((bm, D), lambda t, b, e, lo, hi, f: (b[t], 0)),
                  pl.BlockSpec((1, D), lambda t, b, e, lo, hi, f: (0, 0)),
                  pl.BlockSpec((1, D, D_EXPERT), lambda t, b, e, lo, hi, f: (e[t], 0, 0)),
                  pl.BlockSpec((1, D, D_EXPERT), lambda t, b, e, lo, hi, f: (e[t], 0, 0)),
                  pl.BlockSpec((1, D_EXPERT, D), lambda t, b, e, lo, hi, f: (e[t], 0, 0))],
        out_specs=pl.BlockSpec((bm, D), lambda t, b, e, lo, hi, f: (b[t], 0)),
    )
    return pl.pallas_call(
        _expert_body,
        grid_spec=grid_spec,
        out_shape=jax.ShapeDtypeStruct((A, D), F32),
        compiler_params=_cparams(("arbitrary",)),
        name="expert_ffn",
    )(blk, eid, lo, hi, first, xs, g_ffn, wg, wu, wd)


def _combine_body(dest_ref, x_ref, gate_ref, gf_ref, ys_hbm, o_ref, buf, sem, *, tc, final):
    def issue(t, carry):
        pltpu.make_async_copy(ys_hbm.at[pl.ds(dest_ref[0, 0, t], 1)], buf.at[0, pl.ds(t, 1)], sem).start()
        pltpu.make_async_copy(ys_hbm.at[pl.ds(dest_ref[0, 1, t], 1)], buf.at[1, pl.ds(t, 1)], sem).start()
        return carry

    def drain(t, carry):
        pltpu.make_async_copy(ys_hbm.at[pl.ds(0, 1)], buf.at[0, pl.ds(0, 1)], sem).wait()
        pltpu.make_async_copy(ys_hbm.at[pl.ds(0, 1)], buf.at[0, pl.ds(0, 1)], sem).wait()
        return carry

    lax.fori_loop(0, tc, issue, 0)
    lax.fori_loop(0, tc, drain, 0)
    g = gate_ref[...]
    out = x_ref[...] + g[:, 0:1] * buf[0] + g[:, 1:2] * buf[1]
    if final:
        out = out * lax.rsqrt(jnp.mean(out * out, axis=-1, keepdims=True) + EPS) * gf_ref[...]
    o_ref[...] = out


def _combine(x2, dest3, gates, ys, g_final, tc, final):
    T, D = x2.shape
    return pl.pallas_call(
        functools.partial(_combine_body, tc=tc, final=final),
        grid=(T // tc,),
        in_specs=[pl.BlockSpec((1, 2, tc), lambda i: (i, 0, 0), memory_space=pltpu.SMEM),
                  pl.BlockSpec((tc, D), lambda i: (i, 0)),
                  pl.BlockSpec((tc, LANES), lambda i: (i, 0)),
                  pl.BlockSpec((1, D), lambda i: (0, 0)),
                  pl.BlockSpec(memory_space=pl.ANY)],
        out_specs=pl.BlockSpec((tc, D), lambda i: (i, 0)),
        out_shape=jax.ShapeDtypeStruct((T, D), F32),
        scratch_shapes=[pltpu.VMEM((2, tc, D), F32), pltpu.SemaphoreType.DMA(())],
        compiler_params=_cparams(("arbitrary",)),
        name="combine",
    )(dest3, x2, gates, g_final, ys)


def _rope_tables(L):
    pos = jnp.arange(L, dtype=F32)

    def cs(dim):
        inv = ROPE_THETA ** (-jnp.arange(0, dim, 2, dtype=F32) / dim)
        ang = pos[:, None] * inv[None, :]
        return jnp.cos(ang), jnp.sin(ang)

    ch, sh = cs(HEAD_DIM)
    ci, si = cs(IDX_DIM)
    zi = jnp.zeros_like(si)
    one = jnp.ones_like(ci)
    scale = HEAD_DIM ** -0.5
    cos128 = jnp.concatenate([ch, ch], axis=1)
    sin128 = jnp.concatenate([-sh, sh], axis=1)
    tabs = [
        cos128 * scale, sin128 * scale,
        cos128, sin128,
        jnp.concatenate([ci, ci, ci, ci], axis=1),
        jnp.concatenate([-si, zi, -si, zi], axis=1),
        jnp.concatenate([zi, si, zi, si], axis=1),
        jnp.concatenate([ci, ci, one, one], axis=1),
        jnp.concatenate([-si, zi, zi, zi], axis=1),
        jnp.concatenate([zi, si, zi, zi], axis=1),
    ]
    return jnp.stack(tabs, axis=0)


def _work_items(counts, A, bm):
    starts = jnp.cumsum(counts) - counts
    ends = starts + counts
    nblk = A // bm
    bounds = jnp.sort(jnp.concatenate([jnp.arange(nblk, dtype=I32) * bm, starts[1:].astype(I32)]))
    nxt = jnp.concatenate([bounds[1:], jnp.array([A], I32)])
    blk = jnp.minimum(bounds // bm, nblk - 1)
    eid = jnp.clip(jnp.searchsorted(ends, bounds, side="right"), 0, N_EXPERTS - 1).astype(I32)
    lo = bounds - blk * bm
    hi = nxt - blk * bm
    first = jnp.concatenate([jnp.ones((1,), I32), (blk[1:] != blk[:-1]).astype(I32)])
    return blk.astype(I32), eid, lo.astype(I32), hi.astype(I32), first


def _pick(n, prefs):
    for p in prefs:
        if n % p == 0:
            return p
    return n


def kernel(x, norm_mix, w_in, conv_w, conv_b, conv_ln_g, conv_ln_b, fox_fb, w_out, norm_ffn,
           w_router_group, b_router_group, w_router_expert, b_router_expert, w_gate, w_up, w_down,
           norm_final):
    B, L, D = x.shape
    T = B * L
    depth = w_in.shape[0]
    k_sel = min(TOPK_MAX, L // 4)
    tabs = _rope_tables(L)
    fox_tq = _pick(L, (256, 128))
    bm = _pick(2 * T, (512, 256))
    td = _pick(T, (256, 128))

    segs_a = [("plain", 0, 2 * D_CONV, 0, 0, (), 1.0),
              ("plain", _O_QF, D_FOX, 1, 0, (), HEAD_DIM ** -0.5),
              ("plain", _O_KF, 2 * D_FOX, 1, D_FOX, (), 1.0)]
    outs_a = [(2 * D_CONV, BF16), (3 * D_FOX, BF16)]
    segs_b = [("rope128", 0, D_DSA, 0, 0, (0, 1), 1.0),
              ("rope128", D_DSA, HEAD_DIM, 1, 0, (2, 3), 1.0),
              ("plain", D_DSA + HEAD_DIM, HEAD_DIM, 1, HEAD_DIM, (), 1.0),
              ("rope64", D_DSA + 2 * HEAD_DIM, IDX_HEADS * IDX_DIM, 2, 0, (4, 5, 6), 1.0),
              ("rope64", D_DSA + 2 * HEAD_DIM + IDX_HEADS * IDX_DIM, LANES, 3, 0, (7, 8, 9), 1.0)]
    outs_b = [(D_DSA, BF16), (2 * HEAD_DIM, BF16), (IDX_HEADS * IDX_DIM, BF16), (LANES, F32)]

    xf = x.reshape(T, D)
    for l in range(depth):
        wl = w_in[l]
        w_a = wl[:, :_O_FL].astype(BF16)
        pad = jnp.zeros((D, LANES - IDX_DIM - IDX_HEADS - FOX_HEADS), F32)
        w_b = jnp.concatenate([wl[:, _O_QD:_O_KI], wl[:, _O_KI:_O_END], wl[:, _O_FL:_O_QD], pad],
                              axis=1).astype(BF16)
        w_fl = jnp.concatenate([wl[:, _O_FL:_O_QD].T, jnp.zeros((8 - FOX_HEADS, D), F32)],
                               axis=0).astype(BF16)
        fb8 = jnp.concatenate([fox_fb[l], jnp.zeros((8 - FOX_HEADS,), F32)]).reshape(8, 1)

        u, fox = _proj(xf, norm_mix[l], w_a, tabs, segs_a, outs_a, L)
        qd, kv, qi, small, flT = _proj(xf, norm_mix[l], w_b, tabs, segs_b, outs_b, L, wt=w_fl)

        c = _fox_cumsum(flT, fb8, B, L)
        c4 = c.reshape(B, 8, L // fox_tq, fox_tq)
        ya = _conv_module(u.reshape(B, L, -1), conv_w[l], conv_b[l], conv_ln_g[l], conv_ln_b[l])
        yb = _fox_attention(fox.reshape(B, L, -1), c4, tq=fox_tq)
        yc = _dsa_attention(qd.reshape(B, L, -1), kv.reshape(B, L, -1), qi.reshape(B, L, -1),
                            small.reshape(B, L, -1), k_sel)

        wr = jnp.concatenate([w_router_group[l], w_router_expert[l],
                              jnp.zeros((D, LANES - N_GROUPS - N_EXPERTS), F32)], axis=1)
        wr_hi = wr.astype(BF16)
        wr_lo = (wr - wr_hi.astype(F32)).astype(BF16)
        br = jnp.concatenate([b_router_group[l], b_router_expert[l],
                              jnp.zeros((LANES - N_GROUPS - N_EXPERTS,), F32)]).reshape(1, LANES)
        g_ffn = norm_ffn[l].reshape(1, D)
        x2, logits = _outproj(xf, ya.reshape(T, -1), yb.reshape(T, -1), yc.reshape(T, -1),
                              w_out[l].astype(BF16), g_ffn, wr_hi, wr_lo, br)

        info, gates, cnt = _route(logits)
        counts = cnt[0, :N_EXPERTS].astype(I32)
        starts = jnp.cumsum(counts) - counts
        dest = starts[info[:, 0:2]] + info[:, 2:4]
        dest3 = dest.reshape(T // td, td, 2).transpose(0, 2, 1)

        xs = _dispatch(x2, dest3, td)
        items = _work_items(counts, 2 * T, bm)
        ys = _expert_ffn(xs, g_ffn, w_gate[l].astype(BF16), w_up[l].astype(BF16),
                         w_down[l].astype(BF16), items, bm)
        xf = _combine(x2, dest3, gates, ys, norm_final.reshape(1, D), td, final=(l == depth - 1))
    return xf.reshape(B, L, D)
```

```python
import functools

import jax
import jax.numpy as jnp
import numpy as np
from jax import lax
from jax.experimental import pallas as pl
from jax.experimental.pallas import tpu as pltpu

F32 = jnp.float32
BF16 = jnp.bfloat16
I32 = jnp.int32

D_MODEL = 2048
HEAD_DIM = 128
D_CONV = 512
CONV_WIDTH = 31
FOX_HEADS = 6
DSA_HEADS = 6
D_FOX = FOX_HEADS * HEAD_DIM
D_DSA = DSA_HEADS * HEAD_DIM
IDX_HEADS = 16
IDX_DIM = 64
TOPK_MAX = 256
ROPE_THETA = 10000.0
CHUNK = 64
N_GROUPS = 4
EXPERTS_PER_GROUP = 8
N_EXPERTS = N_GROUPS * EXPERTS_PER_GROUP
D_EXPERT = 512
EPS = 1e-6

LANES = 128
NEG = -1e30
INT_MIN = -(2 ** 31)
VMEM_LIMIT = 56 * 1024 * 1024

_O_U = 0
_O_QF = _O_U + 2 * D_CONV
_O_KF = _O_QF + D_FOX
_O_VF = _O_KF + D_FOX
_O_FL = _O_VF + D_FOX
_O_QD = _O_FL + FOX_HEADS
_O_KD = _O_QD + D_DSA
_O_VD = _O_KD + HEAD_DIM
_O_QI = _O_VD + HEAD_DIM
_O_KI = _O_QI + IDX_HEADS * IDX_DIM
_O_WI = _O_KI + IDX_DIM
_O_END = _O_WI + IDX_HEADS

_S_KI = 0
_S_WI = IDX_DIM
_S_FL = IDX_DIM + IDX_HEADS


def _cparams(sem):
    return pltpu.CompilerParams(dimension_semantics=sem, vmem_limit_bytes=VMEM_LIMIT)


def _proj_body(x_ref, g_ref, w_ref, tab_ref, *rest, segs, has_t):
    if has_t:
        wt_ref, outs = rest[0], rest[1:]
    else:
        wt_ref, outs = None, rest
    x = x_ref[...]
    inv = lax.rsqrt(jnp.mean(x * x, axis=-1, keepdims=True) + EPS)
    h = (x * inv * g_ref[...]).astype(BF16)
    for (kind, w0, width, oi, o0, tabs, scale) in segs:
        z = jnp.dot(h, w_ref[:, w0:w0 + width], preferred_element_type=F32)
        for c in range(width // LANES):
            zc = z[:, c * LANES:(c + 1) * LANES]
            if kind == "plain":
                if scale != 1.0:
                    zc = zc * scale
            elif kind == "rope128":
                zc = zc * tab_ref[tabs[0]] + pltpu.roll(zc, 64, 1) * tab_ref[tabs[1]]
            else:
                zc = (zc * tab_ref[tabs[0]] + pltpu.roll(zc, 96, 1) * tab_ref[tabs[1]]
                      + pltpu.roll(zc, 32, 1) * tab_ref[tabs[2]])
            outs[oi][:, o0 + c * LANES:o0 + (c + 1) * LANES] = zc.astype(outs[oi].dtype)
    if has_t:
        outs[-1][...] = lax.dot_general(wt_ref[...], h, (((1,), (1,)), ((), ())),
                                        preferred_element_type=F32)


def _proj(x2d, g, w, tabs, segs, out_defs, seq_len, wt=None, tm=512):
    T, D = x2d.shape
    tm = min(tm, seq_len)
    nt = T // tm
    nl = seq_len // tm
    ntab = tabs.shape[0]
    in_specs = [
        pl.BlockSpec((tm, D), lambda i: (i, 0)),
        pl.BlockSpec((1, D), lambda i: (0, 0)),
        pl.BlockSpec(w.shape, lambda i: (0, 0)),
        pl.BlockSpec((ntab, tm, LANES), lambda i: (0, i % nl, 0)),
    ]
    args = [x2d, g.reshape(1, D), w, tabs]
    out_shape = [jax.ShapeDtypeStruct((T, n), dt) for (n, dt) in out_defs]
    out_specs = [pl.BlockSpec((tm, n), lambda i: (i, 0)) for (n, dt) in out_defs]
    if wt is not None:
        in_specs.append(pl.BlockSpec(wt.shape, lambda i: (0, 0)))
        args.append(wt)
        out_shape.append(jax.ShapeDtypeStruct((wt.shape[0], T), F32))
        out_specs.append(pl.BlockSpec((wt.shape[0], tm), lambda i: (0, i)))
    return pl.pallas_call(
        functools.partial(_proj_body, segs=tuple(segs), has_t=wt is not None),
        grid=(nt,),
        in_specs=in_specs,
        out_specs=out_specs,
        out_shape=out_shape,
        compiler_params=_cparams(("parallel",)),
        name="proj",
    )(*args)


def _cumsum_body(fl_ref, fb_ref, o_ref):
    z = fl_ref[...] + fb_ref[...]
    x = jnp.minimum(z, 0.0) - jnp.log(1.0 + jnp.exp(-jnp.abs(z)))
    L = x.shape[1]
    lane = lax.broadcasted_iota(I32, x.shape, 1)
    s = 1
    while s < L:
        x = x + jnp.where(lane >= s, pltpu.roll(x, s, 1), 0.0)
        s *= 2
    o_ref[0] = x


def _fox_cumsum(flT, fb8, B, L):
    return pl.pallas_call(
        _cumsum_body,
        grid=(B,),
        in_specs=[pl.BlockSpec((8, L), lambda b: (0, b)),
                  pl.BlockSpec((8, 1), lambda b: (0, 0))],
        out_specs=pl.BlockSpec((1, 8, L), lambda b: (b, 0, 0)),
        out_shape=jax.ShapeDtypeStruct((B, 8, L), F32),
        compiler_params=_cparams(("parallel",)),
        name="fox_cumsum",
    )(flT, fb8)


_CONV_HALO = 32


def _conv_body(u_ref, w_ref, cb_ref, g_ref, b_ref, o_ref, a_scr, *, rc):
    L = u_ref.shape[1]
    a_scr[0:_CONV_HALO, :] = jnp.zeros((_CONV_HALO, D_CONV), F32)
    u1 = u_ref[0, :, 0:D_CONV].astype(F32)
    u2 = u_ref[0, :, D_CONV:2 * D_CONV].astype(F32)
    a_scr[_CONV_HALO:_CONV_HALO + L, :] = u1 * jax.nn.sigmoid(u2)
    win = rc + _CONV_HALO

    def chunk(r, carry):
        base = pl.multiple_of(r * rc, 8)
        wnd = a_scr[pl.ds(base, win), :]
        acc = jnp.zeros((rc, D_CONV), F32) + cb_ref[...]
        for j in range(CONV_WIDTH):
            off = _CONV_HALO - (CONV_WIDTH - 1) + j
            sh = pltpu.roll(wnd, win - off, 0)[0:rc]
            acc = acc + sh * w_ref[j:j + 1, :]
        mu = jnp.mean(acc, axis=-1, keepdims=True)
        d = acc - mu
        var = jnp.mean(d * d, axis=-1, keepdims=True)
        y = d * lax.rsqrt(var + EPS) * g_ref[...] + b_ref[...]
        o_ref[0, pl.ds(base, rc), :] = (y * jax.nn.sigmoid(y)).astype(o_ref.dtype)
        return carry

    lax.fori_loop(0, L // rc, chunk, 0)


def _conv_module(u, conv_w, conv_b, ln_g, ln_b, rc=256):
    B, L, _ = u.shape
    rc = min(rc, L)
    return pl.pallas_call(
        functools.partial(_conv_body, rc=rc),
        grid=(B,),
        in_specs=[pl.BlockSpec((1, L, 2 * D_CONV), lambda b: (b, 0, 0)),
                  pl.BlockSpec((CONV_WIDTH, D_CONV), lambda b: (0, 0)),
                  pl.BlockSpec((1, D_CONV), lambda b: (0, 0)),
                  pl.BlockSpec((1, D_CONV), lambda b: (0, 0)),
                  pl.BlockSpec((1, D_CONV), lambda b: (0, 0))],
        out_specs=pl.BlockSpec((1, L, D_CONV), lambda b: (b, 0, 0)),
        out_shape=jax.ShapeDtypeStruct((B, L, D_CONV), BF16),
        scratch_shapes=[pltpu.VMEM((L + _CONV_HALO, D_CONV), F32)],
        compiler_params=_cparams(("parallel",)),
        name="conv_module",
    )(u, conv_w, conv_b.reshape(1, -1), ln_g.reshape(1, -1), ln_b.reshape(1, -1))


def _fox_body(q_ref, k_ref, v_ref, c_ref, o_ref, *, tq):
    qi = pl.program_id(2)
    q = q_ref[0]

    def tile(j, carry, diag):
        m, l, acc = carry
        off = pl.multiple_of(j * tq, tq)
        k = k_ref[0, pl.ds(off, tq), :]
        v = v_ref[0, pl.ds(off, tq), :]
        s = lax.dot_general(q, k, (((1,), (1,)), ((), ())), preferred_element_type=F32)
        s = s - c_ref[0, 0, pl.ds(j, 1), :]
        if diag:
            row = lax.broadcasted_iota(I32, s.shape, 0)
            col = lax.broadcasted_iota(I32, s.shape, 1)
            s = jnp.where(row >= col, s, NEG)
        m_new = jnp.maximum(m, jnp.max(s, axis=-1, keepdims=True))
        alpha = jnp.exp(m - m_new)
        p = jnp.exp(s - m_new)
        l = l * alpha + jnp.sum(p, axis=-1, keepdims=True)
        acc = acc * alpha + jnp.dot(p.astype(BF16), v, preferred_element_type=F32)
        return m_new, l, acc

    init = (jnp.full((tq, 1), NEG, F32), jnp.zeros((tq, 1), F32), jnp.zeros((tq, HEAD_DIM), F32))
    carry = lax.fori_loop(0, qi, functools.partial(tile, diag=False), init)
    m, l, acc = tile(qi, carry, True)
    o_ref[0] = (acc / l).astype(o_ref.dtype)


def _fox_attention(fox, c4, tq=256):
    B, L, _ = fox.shape
    tq = min(tq, L)
    nq = L // tq
    H = FOX_HEADS
    return pl.pallas_call(
        functools.partial(_fox_body, tq=tq),
        grid=(B, H, nq),
        in_specs=[pl.BlockSpec((1, tq, HEAD_DIM), lambda b, h, i: (b, i, h)),
                  pl.BlockSpec((1, L, HEAD_DIM), lambda b, h, i: (b, 0, H + h)),
                  pl.BlockSpec((1, L, HEAD_DIM), lambda b, h, i: (b, 0, 2 * H + h)),
                  pl.BlockSpec((1, 1, nq, tq), lambda b, h, i: (b, h, 0, 0))],
        out_specs=pl.BlockSpec((1, tq, HEAD_DIM), lambda b, h, i: (b, i, h)),
        out_shape=jax.ShapeDtypeStruct((B, L, D_FOX), BF16),
        compiler_params=_cparams(("parallel", "parallel", "arbitrary")),
        name="fox_attention",
    )(fox, fox, fox, c4)


def _dsa_body(qd_ref, qi_ref, sq_ref, k_ref, v_ref, sk_ref, o_ref, key_scr, *, tq, k_sel):
    qb = pl.program_id(1)
    L = k_ref.shape[1]
    H = DSA_HEADS
    wq = sq_ref[0, :, _S_WI:_S_WI + IDX_HEADS] * ((IDX_DIM ** -0.5) * (IDX_HEADS ** -0.5))
    qi = qi_ref[0]

    key_scr[...] = jnp.full(key_scr.shape, INT_MIN, I32)
    row = lax.broadcasted_iota(I32, (tq, tq), 0)
    col = lax.broadcasted_iota(I32, (tq, tq), 1)
    admissible_diag = (col // CHUNK) <= (row // CHUNK)

    def score_tile(j, carry):
        off = pl.multiple_of(j * tq, tq)
        ki = sk_ref[0, pl.ds(off, tq), _S_KI:_S_KI + IDX_DIM].astype(BF16)
        s = jnp.zeros((tq, tq), F32)
        for h in range(IDX_HEADS):
            r = lax.dot_general(qi[:, h * IDX_DIM:(h + 1) * IDX_DIM], ki,
                                (((1,), (1,)), ((), ())), preferred_element_type=F32)
            s = s + jnp.maximum(r, 0.0) * wq[:, h:h + 1]
        bits = pltpu.bitcast(s, I32)
        key = jnp.where(bits < 0, bits ^ jnp.int32(0x7FFFFFFF), bits)
        key = jnp.where(jnp.logical_or(j < qb, admissible_diag), key, INT_MIN)
        key_scr[j] = key
        return carry

    lax.fori_loop(0, qb + 1, score_tile, 0)

    keys = key_scr[...]

    def bit_step(i, t):
        cand = t + (jnp.int32(1) << (31 - i))
        cnt = jnp.sum(jnp.where(keys >= cand[None], 1.0, 0.0), axis=0)
        cnt = jnp.sum(cnt, axis=-1, keepdims=True)
        return jnp.where(cnt >= float(k_sel), cand, t)

    thr = lax.fori_loop(0, 32, bit_step, jnp.full((tq, 1), INT_MIN, I32))
    thr = jnp.maximum(thr, INT_MIN + 1)

    qs = jnp.concatenate([qd_ref[0, :, h * HEAD_DIM:(h + 1) * HEAD_DIM] for h in range(H)], axis=0)

    def attn_tile(j, carry):
        m, l, acc = carry
        off = pl.multiple_of(j * tq, tq)
        k = k_ref[0, pl.ds(off, tq), :]
        v = v_ref[0, pl.ds(off, tq), :]
        s = lax.dot_general(qs, k, (((1,), (1,)), ((), ())), preferred_element_type=F32)
        sel = key_scr[j] >= thr
        s = jnp.where(sel[None], s.reshape(H, tq, tq), NEG).reshape(H * tq, tq)
        m_new = jnp.maximum(m, jnp.max(s, axis=-1, keepdims=True))
        alpha = jnp.exp(m - m_new)
        p = jnp.exp(s - m_new)
        l = l * alpha + jnp.sum(p, axis=-1, keepdims=True)
        acc = acc * alpha + jnp.dot(p.astype(BF16), v, preferred_element_type=F32)
        return m_new, l, acc

    init = (jnp.full((H * tq, 1), NEG, F32), jnp.zeros((H * tq, 1), F32),
            jnp.zeros((H * tq, HEAD_DIM), F32))
    m, l, acc = lax.fori_loop(0, qb + 1, attn_tile, init)
    out = acc / l
    for h in range(H):
        o_ref[0, :, h * HEAD_DIM:(h + 1) * HEAD_DIM] = out[h * tq:(h + 1) * tq].astype(o_ref.dtype)


def _dsa_attention(qd, kv, qi, small, k_sel, tq=128):
    B, L, _ = qd.shape
    nq = L // tq
    return pl.pallas_call(
        functools.partial(_dsa_body, tq=tq, k_sel=k_sel),
        grid=(B, nq),
        in_specs=[pl.BlockSpec((1, tq, D_DSA), lambda b, i: (b, i, 0)),
                  pl.BlockSpec((1, tq, IDX_HEADS * IDX_DIM), lambda b, i: (b, i, 0)),
                  pl.BlockSpec((1, tq, LANES), lambda b, i: (b, i, 0)),
                  pl.BlockSpec((1, L, HEAD_DIM), lambda b, i: (b, 0, 0)),
                  pl.BlockSpec((1, L, HEAD_DIM), lambda b, i: (b, 0, 1)),
                  pl.BlockSpec((1, L, LANES), lambda b, i: (b, 0, 0))],
        out_specs=pl.BlockSpec((1, tq, D_DSA), lambda b, i: (b, i, 0)),
        out_shape=jax.ShapeDtypeStruct((B, L, D_DSA), BF16),
        scratch_shapes=[pltpu.VMEM((nq, tq, tq), I32)],
        compiler_params=_cparams(("parallel", "arbitrary")),
        name="dsa_attention",
    )(qd, qi, small, kv, kv, small)


def _outproj_body(x_ref, ya_ref, yb_ref, yc_ref, w_ref, g_ref, wrh_ref, wrl_ref, br_ref,
                  x2_ref, lg_ref):
    acc = x_ref[...]
    acc = acc + jnp.dot(ya_ref[...], w_ref[0:D_CONV, :], preferred_element_type=F32)
    acc = acc + jnp.dot(yb_ref[...], w_ref[D_CONV:D_CONV + D_FOX, :], preferred_element_type=F32)
    acc = acc + jnp.dot(yc_ref[...], w_ref[D_CONV + D_FOX:, :], preferred_element_type=F32)
    x2_ref[...] = acc
    h = acc * lax.rsqrt(jnp.mean(acc * acc, axis=-1, keepdims=True) + EPS) * g_ref[...]
    hh = h.astype(BF16)
    hl = (h - hh.astype(F32)).astype(BF16)
    lg = jnp.dot(hh, wrh_ref[...], preferred_element_type=F32)
    lg = lg + jnp.dot(hh, wrl_ref[...], preferred_element_type=F32)
    lg = lg + jnp.dot(hl, wrh_ref[...], preferred_element_type=F32)
    lg_ref[...] = lg + br_ref[...]


def _outproj(x2d, ya, yb, yc, w_out, g_ffn, wr_hi, wr_lo, br, tm=512):
    T, D = x2d.shape
    tm = min(tm, T)
    row = lambda n: pl.BlockSpec((tm, n), lambda i: (i, 0))
    const = lambda a: pl.BlockSpec(a.shape, lambda i: (0, 0))
    return pl.pallas_call(
        _outproj_body,
        grid=(T // tm,),
        in_specs=[row(D), row(D_CONV), row(D_FOX), row(D_DSA), const(w_out), const(g_ffn),
                  const(wr_hi), const(wr_lo), const(br)],
        out_specs=[row(D), row(LANES)],
        out_shape=[jax.ShapeDtypeStruct((T, D), F32), jax.ShapeDtypeStruct((T, LANES), F32)],
        compiler_params=_cparams(("parallel",)),
        name="outproj",
    )(x2d, ya, yb, yc, w_out, g_ffn, wr_hi, wr_lo, br)


def _route_body(lg_ref, info_ref, gate_ref, cnt_ref):
    i = pl.program_id(0)
    lg = lg_ref[...]
    tm = lg.shape[0]
    lane = lax.broadcasted_iota(I32, lg.shape, 1)
    ninf = -jnp.inf

    lane_f = lane.astype(F32)

    def first_lane(mask):
        return jnp.min(jnp.where(mask, lane_f, float(LANES)), axis=-1, keepdims=True).astype(I32)

    gl = jnp.where(lane < N_GROUPS, lg, ninf)
    gexp = jnp.exp(gl - jnp.max(gl, axis=-1, keepdims=True))
    gprob = gexp / jnp.sum(gexp, axis=-1, keepdims=True)
    p_g = jnp.max(gprob, axis=-1, keepdims=True)
    g_idx = first_lane(gprob == p_g)

    e_lo = N_GROUPS + EXPERTS_PER_GROUP * g_idx
    emask = jnp.logical_and(lane >= e_lo, lane < e_lo + EXPERTS_PER_GROUP)
    el = jnp.where(emask, lg, ninf)
    eexp = jnp.exp(el - jnp.max(el, axis=-1, keepdims=True))
    eprob = jnp.where(emask, eexp / jnp.sum(eexp, axis=-1, keepdims=True), -1.0)
    p1 = jnp.max(eprob, axis=-1, keepdims=True)
    l1 = first_lane(eprob == p1)
    eprob2 = jnp.where(lane == l1, -1.0, eprob)
    p2 = jnp.max(eprob2, axis=-1, keepdims=True)
    l2 = first_lane(eprob2 == p2)
    den = p1 + p2
    gate1 = p_g * p1 / den
    gate2 = p_g * p2 / den
    eid1 = l1 - N_GROUPS
    eid2 = l2 - N_GROUPS

    @pl.when(i == 0)
    def _():
        cnt_ref[...] = jnp.zeros(cnt_ref.shape, F32)

    hot1 = lane == eid1
    hot2 = lane == eid2
    onehot = jnp.where(jnp.logical_or(hot1, hot2), 1.0, 0.0)
    r = lax.broadcasted_iota(I32, (tm, tm), 0)
    c = lax.broadcasted_iota(I32, (tm, tm), 1)
    tri = jnp.where(c < r, 1.0, 0.0).astype(BF16)
    before = jnp.dot(tri, onehot.astype(BF16), preferred_element_type=F32) + cnt_ref[0:1, :]
    rank1 = jnp.sum(jnp.where(hot1, before, 0.0), axis=-1, keepdims=True).astype(I32)
    rank2 = jnp.sum(jnp.where(hot2, before, 0.0), axis=-1, keepdims=True).astype(I32)
    cnt_ref[...] = cnt_ref[...] + jnp.sum(onehot, axis=0, keepdims=True)

    info_ref[...] = jnp.where(lane == 0, eid1, jnp.where(lane == 1, eid2,
                              jnp.where(lane == 2, rank1, jnp.where(lane == 3, rank2, 0))))
    gate_ref[...] = jnp.where(lane == 0, gate1, jnp.where(lane == 1, gate2, 0.0))


def _route(logits, tm=512):
    T = logits.shape[0]
    tm = min(tm, T)
    row = pl.BlockSpec((tm, LANES), lambda i: (i, 0))
    return pl.pallas_call(
        _route_body,
        grid=(T // tm,),
        in_specs=[row],
        out_specs=[row, row, pl.BlockSpec((8, LANES), lambda i: (0, 0))],
        out_shape=[jax.ShapeDtypeStruct((T, LANES), I32), jax.ShapeDtypeStruct((T, LANES), F32),
                   jax.ShapeDtypeStruct((8, LANES), F32)],
        compiler_params=_cparams(("arbitrary",)),
        name="route",
    )(logits)


def _row_copy(src, s, dst, d, sem):
    return pltpu.make_async_copy(src.at[pl.ds(s, 1)], dst.at[pl.ds(d, 1)], sem)


def _dispatch_body(dest_ref, x_ref, xs_hbm, sem, *, td):
    def issue(t, carry):
        _row_copy(x_ref, t, xs_hbm, dest_ref[0, 0, t], sem).start()
        _row_copy(x_ref, t, xs_hbm, dest_ref[0, 1, t], sem).start()
        return carry

    def drain(t, carry):
        _row_copy(x_ref, 0, xs_hbm, 0, sem).wait()
        _row_copy(x_ref, 0, xs_hbm, 0, sem).wait()
        return carry

    lax.fori_loop(0, td, issue, 0)
    lax.fori_loop(0, td, drain, 0)


def _dispatch(x2, dest3, td):
    T, D = x2.shape
    return pl.pallas_call(
        functools.partial(_dispatch_body, td=td),
        grid=(T // td,),
        in_specs=[pl.BlockSpec((1, 2, td), lambda i: (i, 0, 0), memory_space=pltpu.SMEM),
                  pl.BlockSpec((td, D), lambda i: (i, 0))],
        out_specs=pl.BlockSpec(memory_space=pl.ANY),
        out_shape=jax.ShapeDtypeStruct((2 * T, D), F32),
        scratch_shapes=[pltpu.SemaphoreType.DMA(())],
        compiler_params=_cparams(("arbitrary",)),
        name="dispatch",
    )(dest3, x2)


def _expert_body(blk_ref, e_ref, lo_ref, hi_ref, first_ref, xs_ref, g_ref, wg_ref, wu_ref, wd_ref,
                 o_ref):
    t = pl.program_id(0)
    lo = lo_ref[t]
    hi = hi_ref[t]

    @pl.when(first_ref[t] == 1)
    def _():
        o_ref[...] = jnp.zeros(o_ref.shape, F32)

    @pl.when(hi > lo)
    def _():
        x = xs_ref[...]
        h = (x * lax.rsqrt(jnp.mean(x * x, axis=-1, keepdims=True) + EPS) * g_ref[...]).astype(BF16)
        g = jnp.dot(h, wg_ref[0], preferred_element_type=F32)
        u = jnp.dot(h, wu_ref[0], preferred_element_type=F32)
        hdn = (g * jax.nn.sigmoid(g) * u).astype(BF16)
        y = jnp.dot(hdn, wd_ref[0], preferred_element_type=F32)
        row = lax.broadcasted_iota(I32, (y.shape[0], 1), 0)
        keep = jnp.logical_and(row >= lo, row < hi)
        o_ref[...] = o_ref[...] + jnp.where(keep, y, 0.0)


def _expert_ffn(xs, g_ffn, wg, wu, wd, items, bm):
    A, D = xs.shape
    blk, eid, lo, hi, first = items
    ni = blk.shape[0]
    grid_spec = pltpu.PrefetchScalarGridSpec(
        num_scalar_prefetch=5,
        grid=(ni,),
        in_specs=[pl.BlockSpec((bm, D), lambda t, b, e, lo, hi, f: (b[t], 0)),
                  pl.BlockSpec((1, D), lambda t, b, e, lo, hi, f: (0, 0)),
                  pl.BlockSpec((1, D, D_EXPERT), lambda t, b, e, lo, hi, f: (e[t], 0, 0)),
                  pl.BlockSpec((1, D, D_EXPERT), lambda t, b, e, lo, hi, f: (e[t], 0, 0)),
                  pl.BlockSpec((1, D_EXPERT, D), lambda t, b, e, lo, hi, f: (e[t], 0, 0))],
        out_specs=pl.BlockSpec((bm, D), lambda t, b, e, lo, hi, f: (b[t], 0)),
    )
    return pl.pallas_call(
        _expert_body,
        grid_spec=grid_spec,
        out_shape=jax.ShapeDtypeStruct((A, D), F32),
        compiler_params=_cparams(("arbitrary",)),
        name="expert_ffn",
    )(blk, eid, lo, hi, first, xs, g_ffn, wg, wu, wd)


def _combine_body(dest_ref, x_ref, gate_ref, gf_ref, ys_hbm, o_ref, buf, sem, *, tc, final):
    def issue(t, carry):
        pltpu.make_async_copy(ys_hbm.at[pl.ds(dest_ref[0, 0, t], 1)], buf.at[0, pl.ds(t, 1)], sem).start()
        pltpu.make_async_copy(ys_hbm.at[pl.ds(dest_ref[0, 1, t], 1)], buf.at[1, pl.ds(t, 1)], sem).start()
        return carry

    def drain(t, carry):
        pltpu.make_async_copy(ys_hbm.at[pl.ds(0, 1)], buf.at[0, pl.ds(0, 1)], sem).wait()
        pltpu.make_async_copy(ys_hbm.at[pl.ds(0, 1)], buf.at[0, pl.ds(0, 1)], sem).wait()
        return carry

    lax.fori_loop(0, tc, issue, 0)
    lax.fori_loop(0, tc, drain, 0)
    g = gate_ref[...]
    out = x_ref[...] + g[:, 0:1] * buf[0] + g[:, 1:2] * buf[1]
    if final:
        out = out * lax.rsqrt(jnp.mean(out * out, axis=-1, keepdims=True) + EPS) * gf_ref[...]
    o_ref[...] = out


def _combine(x2, dest3, gates, ys, g_final, tc, final):
    T, D = x2.shape
    return pl.pallas_call(
        functools.partial(_combine_body, tc=tc, final=final),
        grid=(T // tc,),
        in_specs=[pl.BlockSpec((1, 2, tc), lambda i: (i, 0, 0), memory_space=pltpu.SMEM),
                  pl.BlockSpec((tc, D), lambda i: (i, 0)),
                  pl.BlockSpec((tc, LANES), lambda i: (i, 0)),
                  pl.BlockSpec((1, D), lambda i: (0, 0)),
                  pl.BlockSpec(memory_space=pl.ANY)],
        out_specs=pl.BlockSpec((tc, D), lambda i: (i, 0)),
        out_shape=jax.ShapeDtypeStruct((T, D), F32),
        scratch_shapes=[pltpu.VMEM((2, tc, D), F32), pltpu.SemaphoreType.DMA(())],
        compiler_params=_cparams(("arbitrary",)),
        name="combine",
    )(dest3, x2, gates, g_final, ys)


def _rope_tables(L):
    pos = jnp.arange(L, dtype=F32)

    def cs(dim):
        inv = ROPE_THETA ** (-jnp.arange(0, dim, 2, dtype=F32) / dim)
        ang = pos[:, None] * inv[None, :]
        return jnp.cos(ang), jnp.sin(ang)

    ch, sh = cs(HEAD_DIM)
    ci, si = cs(IDX_DIM)
    zi = jnp.zeros_like(si)
    one = jnp.ones_like(ci)
    scale = HEAD_DIM ** -0.5
    cos128 = jnp.concatenate([ch, ch], axis=1)
    sin128 = jnp.concatenate([-sh, sh], axis=1)
    tabs = [
        cos128 * scale, sin128 * scale,
        cos128, sin128,
        jnp.concatenate([ci, ci, ci, ci], axis=1),
        jnp.concatenate([-si, zi, -si, zi], axis=1),
        jnp.concatenate([zi, si, zi, si], axis=1),
        jnp.concatenate([ci, ci, one, one], axis=1),
        jnp.concatenate([-si, zi, zi, zi], axis=1),
        jnp.concatenate([zi, si, zi, zi], axis=1),
    ]
    return jnp.stack(tabs, axis=0)


def _work_items(counts, A, bm):
    starts = jnp.cumsum(counts) - counts
    ends = starts + counts
    nblk = A // bm
    bstart = jnp.arange(nblk, dtype=I32) * bm
    estart = starts[1:].astype(I32)
    bpos = jnp.arange(nblk, dtype=I32) + jnp.sum(estart[None, :] < bstart[:, None], axis=1).astype(I32)
    epos = jnp.arange(N_EXPERTS - 1, dtype=I32) + jnp.minimum(estart // bm + 1, nblk)
    bounds = jnp.zeros((nblk + N_EXPERTS - 1,), I32).at[bpos].set(bstart).at[epos].set(estart)
    nxt = jnp.concatenate([bounds[1:], jnp.array([A], I32)])
    blk = jnp.minimum(bounds // bm, nblk - 1)
    eid = jnp.minimum(jnp.sum(ends[None, :] <= bounds[:, None], axis=1), N_EXPERTS - 1).astype(I32)
    lo = bounds - blk * bm
    hi = nxt - blk * bm
    first = jnp.concatenate([jnp.ones((1,), I32), (blk[1:] != blk[:-1]).astype(I32)])
    return blk.astype(I32), eid, lo.astype(I32), hi.astype(I32), first


def _pick(n, prefs):
    for p in prefs:
        if n % p == 0:
            return p
    return n


def kernel(x, norm_mix, w_in, conv_w, conv_b, conv_ln_g, conv_ln_b, fox_fb, w_out, norm_ffn,
           w_router_group, b_router_group, w_router_expert, b_router_expert, w_gate, w_up, w_down,
           norm_final):
    B, L, D = x.shape
    T = B * L
    depth = w_in.shape[0]
    k_sel = min(TOPK_MAX, L // 4)
    tabs = _rope_tables(L)
    fox_tq = _pick(L, (256, 128))
    bm = _pick(2 * T, (512, 256))
    td = _pick(T, (256, 128))

    segs_a = [("plain", 0, 2 * D_CONV, 0, 0, (), 1.0),
              ("plain", _O_QF, D_FOX, 1, 0, (), HEAD_DIM ** -0.5),
              ("plain", _O_KF, 2 * D_FOX, 1, D_FOX, (), 1.0)]
    outs_a = [(2 * D_CONV, BF16), (3 * D_FOX, BF16)]
    segs_b = [("rope128", 0, D_DSA, 0, 0, (0, 1), 1.0),
              ("rope128", D_DSA, HEAD_DIM, 1, 0, (2, 3), 1.0),
              ("plain", D_DSA + HEAD_DIM, HEAD_DIM, 1, HEAD_DIM, (), 1.0),
              ("rope64", D_DSA + 2 * HEAD_DIM, IDX_HEADS * IDX_DIM, 2, 0, (4, 5, 6), 1.0),
              ("rope64", D_DSA + 2 * HEAD_DIM + IDX_HEADS * IDX_DIM, LANES, 3, 0, (7, 8, 9), 1.0)]
    outs_b = [(D_DSA, BF16), (2 * HEAD_DIM, BF16), (IDX_HEADS * IDX_DIM, BF16), (LANES, F32)]

    xf = x.reshape(T, D)
    for l in range(depth):
        wl = w_in[l]
        w_a = wl[:, :_O_FL].astype(BF16)
        pad = jnp.zeros((D, LANES - IDX_DIM - IDX_HEADS - FOX_HEADS), F32)
        w_b = jnp.concatenate([wl[:, _O_QD:_O_KI], wl[:, _O_KI:_O_END], wl[:, _O_FL:_O_QD], pad],
                              axis=1).astype(BF16)
        w_fl = jnp.concatenate([wl[:, _O_FL:_O_QD].T, jnp.zeros((8 - FOX_HEADS, D), F32)],
                               axis=0).astype(BF16)
        fb8 = jnp.concatenate([fox_fb[l], jnp.zeros((8 - FOX_HEADS,), F32)]).reshape(8, 1)

        u, fox = _proj(xf, norm_mix[l], w_a, tabs, segs_a, outs_a, L)
        qd, kv, qi, small, flT = _proj(xf, norm_mix[l], w_b, tabs, segs_b, outs_b, L, wt=w_fl)

        c = _fox_cumsum(flT, fb8, B, L)
        c4 = c.reshape(B, 8, L // fox_tq, fox_tq)
        ya = _conv_module(u.reshape(B, L, -1), conv_w[l], conv_b[l], conv_ln_g[l], conv_ln_b[l])
        yb = _fox_attention(fox.reshape(B, L, -1), c4, tq=fox_tq)
        yc = _dsa_attention(qd.reshape(B, L, -1), kv.reshape(B, L, -1), qi.reshape(B, L, -1),
                            small.reshape(B, L, -1), k_sel)

        wr = jnp.concatenate([w_router_group[l], w_router_expert[l],
                              jnp.zeros((D, LANES - N_GROUPS - N_EXPERTS), F32)], axis=1)
        wr_hi = wr.astype(BF16)
        wr_lo = (wr - wr_hi.astype(F32)).astype(BF16)
        br = jnp.concatenate([b_router_group[l], b_router_expert[l],
                              jnp.zeros((LANES - N_GROUPS - N_EXPERTS,), F32)]).reshape(1, LANES)
        g_ffn = norm_ffn[l].reshape(1, D)
        x2, logits = _outproj(xf, ya.reshape(T, -1), yb.reshape(T, -1), yc.reshape(T, -1),
                              w_out[l].astype(BF16), g_ffn, wr_hi, wr_lo, br)

        info, gates, cnt = _route(logits)
        counts = cnt[0, :N_EXPERTS].astype(I32)
        starts = jnp.cumsum(counts) - counts
        dest = starts[info[:, 0:2]] + info[:, 2:4]
        dest3 = dest.reshape(T // td, td, 2).transpose(0, 2, 1)

        xs = _dispatch(x2, dest3, td)
        items = _work_items(counts, 2 * T, bm)
        ys = _expert_ffn(xs, g_ffn, w_gate[l].astype(BF16), w_up[l].astype(BF16),
                         w_down[l].astype(BF16), items, bm)
        xf = _combine(x2, dest3, gates, ys, norm_final.reshape(1, D), td, final=(l == depth - 1))
    return xf.reshape(B, L, D)
```

```python
import functools

import jax
import jax.numpy as jnp
import numpy as np
from jax import lax
from jax.experimental import pallas as pl
from jax.experimental.pallas import tpu as pltpu

F32 = jnp.float32
BF16 = jnp.bfloat16
I32 = jnp.int32

D_MODEL = 2048
HEAD_DIM = 128
D_CONV = 512
CONV_WIDTH = 31
FOX_HEADS = 6
DSA_HEADS = 6
D_FOX = FOX_HEADS * HEAD_DIM
D_DSA = DSA_HEADS * HEAD_DIM
IDX_HEADS = 16
IDX_DIM = 64
TOPK_MAX = 256
ROPE_THETA = 10000.0
CHUNK = 64
N_GROUPS = 4
EXPERTS_PER_GROUP = 8
N_EXPERTS = N_GROUPS * EXPERTS_PER_GROUP
D_EXPERT = 512
EPS = 1e-6

LANES = 128
NEG = -1e30
INT_MIN = -(2 ** 31)
VMEM_LIMIT = 56 * 1024 * 1024

_O_U = 0
_O_QF = _O_U + 2 * D_CONV
_O_KF = _O_QF + D_FOX
_O_VF = _O_KF + D_FOX
_O_FL = _O_VF + D_FOX
_O_QD = _O_FL + FOX_HEADS
_O_KD = _O_QD + D_DSA
_O_VD = _O_KD + HEAD_DIM
_O_QI = _O_VD + HEAD_DIM
_O_KI = _O_QI + IDX_HEADS * IDX_DIM
_O_WI = _O_KI + IDX_DIM
_O_END = _O_WI + IDX_HEADS

_S_KI = 0
_S_WI = IDX_DIM
_S_FL = IDX_DIM + IDX_HEADS


def _cparams(sem):
    return pltpu.CompilerParams(dimension_semantics=sem, vmem_limit_bytes=VMEM_LIMIT)


def _proj_body(x_ref, g_ref, w_ref, tab_ref, *rest, segs, n_t, n_out):
    wts, outs, outs_t = rest[:n_t], rest[n_t:n_t + n_out], rest[n_t + n_out:]
    x = x_ref[...]
    inv = lax.rsqrt(jnp.mean(x * x, axis=-1, keepdims=True) + EPS)
    h = (x * inv * g_ref[...]).astype(BF16)
    for (kind, w0, width, oi, o0, tabs, scale) in segs:
        z = jnp.dot(h, w_ref[:, w0:w0 + width], preferred_element_type=F32)
        for c in range(width // LANES):
            zc = z[:, c * LANES:(c + 1) * LANES]
            if kind == "plain":
                if scale != 1.0:
                    zc = zc * scale
            elif kind == "rope128":
                zc = zc * tab_ref[tabs[0]] + pltpu.roll(zc, 64, 1) * tab_ref[tabs[1]]
            else:
                zc = (zc * tab_ref[tabs[0]] + pltpu.roll(zc, 96, 1) * tab_ref[tabs[1]]
                      + pltpu.roll(zc, 32, 1) * tab_ref[tabs[2]])
            outs[oi][:, o0 + c * LANES:o0 + (c + 1) * LANES] = zc.astype(outs[oi].dtype)
    for wt_ref, o_ref in zip(wts, outs_t):
        zt = lax.dot_general(wt_ref[...], h, (((1,), (1,)), ((), ())), preferred_element_type=F32)
        tw = o_ref.shape[2]
        for c in range(o_ref.shape[0]):
            o_ref[c] = zt[:, c * tw:(c + 1) * tw].astype(o_ref.dtype)


def _proj(x2d, g, w, tabs, segs, out_defs, seq_len, t_groups=(), tm=512):
    T, D = x2d.shape
    tm = min(tm, seq_len)
    nt = T // tm
    nl = seq_len // tm
    ntab = tabs.shape[0]
    in_specs = [
        pl.BlockSpec((tm, D), lambda i: (i, 0)),
        pl.BlockSpec((1, D), lambda i: (0, 0)),
        pl.BlockSpec(w.shape, lambda i: (0, 0)),
        pl.BlockSpec((ntab, tm, LANES), lambda i: (0, i % nl, 0)),
    ]
    args = [x2d, g.reshape(1, D), w, tabs]
    out_shape = [jax.ShapeDtypeStruct((T, n), dt) for (n, dt) in out_defs]
    out_specs = [pl.BlockSpec((tm, n), lambda i: (i, 0)) for (n, dt) in out_defs]
    for (wt, tw, dt) in t_groups:
        in_specs.append(pl.BlockSpec(wt.shape, lambda i: (0, 0)))
        args.append(wt)
    for (wt, tw, dt) in t_groups:
        out_shape.append(jax.ShapeDtypeStruct((T // tw, wt.shape[0], tw), dt))
        out_specs.append(pl.BlockSpec((tm // tw, wt.shape[0], tw), lambda i: (i, 0, 0)))
    return pl.pallas_call(
        functools.partial(_proj_body, segs=tuple(segs), n_t=len(t_groups), n_out=len(out_defs)),
        grid=(nt,),
        in_specs=in_specs,
        out_specs=out_specs,
        out_shape=out_shape,
        compiler_params=_cparams(("parallel",)),
        name="proj",
    )(*args)


def _cumsum_body(fl_ref, fb_ref, o_ref):
    z = fl_ref[0] + fb_ref[...]
    x = jnp.minimum(z, 0.0) - jnp.log(1.0 + jnp.exp(-jnp.abs(z)))
    L = x.shape[0]
    row = lax.broadcasted_iota(I32, x.shape, 0)
    s = 1
    while s < L:
        x = x + jnp.where(row >= s, pltpu.roll(x, s, 0), 0.0)
        s *= 2
    o_ref[0] = x


def _fox_cumsum(small, fb_row):
    B, L, _ = small.shape
    return pl.pallas_call(
        _cumsum_body,
        grid=(B,),
        in_specs=[pl.BlockSpec((1, L, LANES), lambda b: (b, 0, 0)),
                  pl.BlockSpec((1, LANES), lambda b: (0, 0))],
        out_specs=pl.BlockSpec((1, L, LANES), lambda b: (b, 0, 0)),
        out_shape=jax.ShapeDtypeStruct((B, L, LANES), F32),
        compiler_params=_cparams(("parallel",)),
        name="fox_cumsum",
    )(small, fb_row)


_CONV_HALO = 32


def _conv_body(u_ref, w_ref, cb_ref, g_ref, b_ref, o_ref, a_scr, *, rc):
    L = u_ref.shape[1]
    a_scr[0:_CONV_HALO, :] = jnp.zeros((_CONV_HALO, D_CONV), F32)
    u1 = u_ref[0, :, 0:D_CONV].astype(F32)
    u2 = u_ref[0, :, D_CONV:2 * D_CONV].astype(F32)
    a_scr[_CONV_HALO:_CONV_HALO + L, :] = u1 * jax.nn.sigmoid(u2)
    win = rc + _CONV_HALO

    def chunk(r, carry):
        base = pl.multiple_of(r * rc, 8)
        wnd = a_scr[pl.ds(base, win), :]
        acc = jnp.zeros((rc, D_CONV), F32) + cb_ref[...]
        for j in range(CONV_WIDTH):
            off = _CONV_HALO - (CONV_WIDTH - 1) + j
            sh = pltpu.roll(wnd, win - off, 0)[0:rc]
            acc = acc + sh * w_ref[j:j + 1, :]
        mu = jnp.mean(acc, axis=-1, keepdims=True)
        d = acc - mu
        var = jnp.mean(d * d, axis=-1, keepdims=True)
        y = d * lax.rsqrt(var + EPS) * g_ref[...] + b_ref[...]
        o_ref[0, pl.ds(base, rc), :] = (y * jax.nn.sigmoid(y)).astype(o_ref.dtype)
        return carry

    lax.fori_loop(0, L // rc, chunk, 0)


def _conv_module(u, conv_w, conv_b, ln_g, ln_b, rc=256):
    B, L, _ = u.shape
    rc = min(rc, L)
    return pl.pallas_call(
        functools.partial(_conv_body, rc=rc),
        grid=(B,),
        in_specs=[pl.BlockSpec((1, L, 2 * D_CONV), lambda b: (b, 0, 0)),
                  pl.BlockSpec((CONV_WIDTH, D_CONV), lambda b: (0, 0)),
                  pl.BlockSpec((1, D_CONV), lambda b: (0, 0)),
                  pl.BlockSpec((1, D_CONV), lambda b: (0, 0)),
                  pl.BlockSpec((1, D_CONV), lambda b: (0, 0))],
        out_specs=pl.BlockSpec((1, L, D_CONV), lambda b: (b, 0, 0)),
        out_shape=jax.ShapeDtypeStruct((B, L, D_CONV), BF16),
        scratch_shapes=[pltpu.VMEM((L + _CONV_HALO, D_CONV), F32)],
        compiler_params=_cparams(("parallel",)),
        name="conv_module",
    )(u, conv_w, conv_b.reshape(1, -1), ln_g.reshape(1, -1), ln_b.reshape(1, -1))


_ATT_TK = 128
_FOX_TK = 128


def _fox_body(q_ref, k_ref, vt_ref, c_ref, o_ref, cb_scr, acc_scr, *, tq, tk):
    qi = pl.program_id(1)
    L = k_ref.shape[1]
    H = FOX_HEADS
    nsub = tq // tk

    @pl.when(qi == 0)
    def _():
        def fill(r, carry):
            off = pl.multiple_of(r * tk, tk)
            c = c_ref[0, pl.ds(off, tk), :]
            for h in range(H):
                cb_scr[h, pl.ds(off, tk), :] = jnp.broadcast_to(
                    c[:, _S_FL + h:_S_FL + h + 1], (tk, LANES))
            return carry

        lax.fori_loop(0, L // tk, fill, 0)

    q = [q_ref[0, :, h * HEAD_DIM:(h + 1) * HEAD_DIM] for h in range(H)]
    acc_scr[...] = jnp.zeros(acc_scr.shape, F32)

    def tile(j, carry, diag):
        ms, ls = carry
        off = pl.multiple_of(j * tk, tk)
        new_m, new_l = [], []
        for h in range(H):
            k = k_ref[0, pl.ds(off, tk), h * HEAD_DIM:(h + 1) * HEAD_DIM]
            s = lax.dot_general(k, q[h], (((1,), (1,)), ((), ())), preferred_element_type=F32)
            cb = cb_scr[h, pl.ds(off, tk), :]
            s = s - jnp.concatenate([cb] * (tq // LANES), axis=1)
            if diag:
                kk = off + lax.broadcasted_iota(I32, s.shape, 0)
                qq = qi * tq + lax.broadcasted_iota(I32, s.shape, 1)
                s = jnp.where(kk <= qq, s, NEG)
            m_new = jnp.maximum(ms[h], jnp.max(s, axis=0, keepdims=True))
            alpha = jnp.exp(ms[h] - m_new)
            p = jnp.exp(s - m_new)
            new_l.append(ls[h] * alpha + jnp.sum(p, axis=0, keepdims=True))
            new_m.append(m_new)
            pv = jnp.dot(vt_ref[0, j, h * HEAD_DIM:(h + 1) * HEAD_DIM, :], p.astype(BF16),
                         preferred_element_type=F32)
            acc_scr[h] = acc_scr[h] * alpha + pv
        return tuple(new_m), tuple(new_l)

    init = (tuple(jnp.full((1, tq), NEG, F32) for _ in range(H)),
            tuple(jnp.zeros((1, tq), F32) for _ in range(H)))
    carry = lax.fori_loop(0, qi * nsub, functools.partial(tile, diag=False), init)
    for d in range(nsub):
        carry = tile(qi * nsub + d, carry, True)
    ms, ls = carry
    for h in range(H):
        out = acc_scr[h] * (1.0 / ls[h])
        o_ref[0, :, h * HEAD_DIM:(h + 1) * HEAD_DIM] = out.T.astype(o_ref.dtype)


def _fox_attention(qk, vt, c_tm, tq=256):
    B, L, _ = qk.shape
    tq = min(tq, L)
    nq = L // tq
    nk, tk = vt.shape[1], vt.shape[3]
    return pl.pallas_call(
        functools.partial(_fox_body, tq=tq, tk=tk),
        grid=(B, nq),
        in_specs=[pl.BlockSpec((1, tq, D_FOX), lambda b, i: (b, i, 0)),
                  pl.BlockSpec((1, L, D_FOX), lambda b, i: (b, 0, 1)),
                  pl.BlockSpec((1, nk, D_FOX, tk), lambda b, i: (b, 0, 0, 0)),
                  pl.BlockSpec((1, L, LANES), lambda b, i: (b, 0, 0))],
        out_specs=pl.BlockSpec((1, tq, D_FOX), lambda b, i: (b, i, 0)),
        out_shape=jax.ShapeDtypeStruct((B, L, D_FOX), BF16),
        scratch_shapes=[pltpu.VMEM((FOX_HEADS, L, LANES), F32),
                        pltpu.VMEM((FOX_HEADS, HEAD_DIM, tq), F32)],
        compiler_params=_cparams(("arbitrary", "arbitrary")),
        name="fox_attention",
    )(qk, qk, vt, c_tm)


def _dsa_body(qd_ref, qi_ref, wt_ref, k_ref, kdup_ref, vt_ref, o_ref,
              key_scr, rhs_scr, acc_scr, thr_scr, *, tq, k_sel, nvs):
    qb = pl.program_id(1)
    H = DSA_HEADS
    half = IDX_DIM

    lane = lax.broadcasted_iota(I32, (tq, LANES), 1)
    for p in range(IDX_HEADS // 2):
        qp = qi_ref[0, :, p * LANES:(p + 1) * LANES].astype(F32)
        rhs_scr[(2 * p) * tq:(2 * p + 1) * tq, :] = jnp.where(lane < half, qp, 0.0).astype(BF16)
        rhs_scr[(2 * p + 1) * tq:(2 * p + 2) * tq, :] = jnp.where(lane >= half, qp, 0.0).astype(BF16)
    wq = wt_ref[0] * ((IDX_DIM ** -0.5) * (IDX_HEADS ** -0.5))

    key_scr[...] = jnp.full(key_scr.shape, INT_MIN, I32)
    rowk = lax.broadcasted_iota(I32, (tq, tq), 0)
    colq = lax.broadcasted_iota(I32, (tq, tq), 1)
    admissible_diag = (rowk // CHUNK) <= (colq // CHUNK)

    npair = (qb + 2) // 2

    def score_pair(jj, carry):
        for g in range(2):
            j = 2 * jj + g
            off = pl.multiple_of(j * tq, tq)
            kd = kdup_ref[0, pl.ds(off, tq), :]
            s = jnp.zeros((tq, tq), F32)
            for p in range(IDX_HEADS // 2):
                r = lax.dot_general(kd, rhs_scr[(2 * p) * tq:(2 * p + 2) * tq, :],
                                    (((1,), (1,)), ((), ())), preferred_element_type=F32)
                s = s + jnp.maximum(r[:, :tq], 0.0) * wq[2 * p:2 * p + 1, :]
                s = s + jnp.maximum(r[:, tq:], 0.0) * wq[2 * p + 1:2 * p + 2, :]
            bits = pltpu.bitcast(s, I32)
            key = jnp.where(bits < 0, bits ^ jnp.int32(0x7FFFFFFF), bits)
            adm = jnp.logical_or(j < qb, jnp.logical_and(j == qb, admissible_diag))
            key_scr[j] = jnp.where(adm, key, INT_MIN)
        return carry

    lax.fori_loop(0, npair, score_pair, 0)

    def select(nv):
        def bit_step(i, t):
            cand = t + (jnp.int32(1) << (31 - i))
            cnt = jnp.sum(jnp.where(key_scr[0:nv] >= cand[None], 1.0, 0.0), axis=0)
            cnt = jnp.sum(cnt, axis=0, keepdims=True)
            return jnp.where(cnt >= float(k_sel), cand, t)

        t = lax.fori_loop(0, 32, bit_step, jnp.full((1, tq), INT_MIN, I32))
        t = jnp.maximum(t, INT_MIN + 1)
        thr_scr[...] = jnp.broadcast_to(t, thr_scr.shape)

    for idx, nv in enumerate(nvs):
        lo = nvs[idx - 1] if idx else 0

        @pl.when(jnp.logical_and(qb + 1 > lo, qb + 1 <= nv))
        def _(nv=nv):
            select(nv)

    thr = thr_scr[0:1, :]
    qs = jnp.concatenate([qd_ref[0, :, h * HEAD_DIM:(h + 1) * HEAD_DIM] for h in range(H)], axis=0)
    acc_scr[...] = jnp.zeros(acc_scr.shape, F32)

    def attn_pair(jj, carry):
        m, l = carry
        off = pl.multiple_of(jj * (2 * tq), 2 * tq)
        k = k_ref[0, pl.ds(off, 2 * tq), :]
        s = lax.dot_general(k, qs, (((1,), (1,)), ((), ())), preferred_element_type=F32)
        sel = key_scr[pl.ds(2 * jj, 2)].reshape(2 * tq, tq) >= thr
        s = jnp.concatenate([jnp.where(sel, s[:, h * tq:(h + 1) * tq], NEG) for h in range(H)], axis=1)
        m_new = jnp.maximum(m, jnp.max(s, axis=0, keepdims=True))
        alpha = jnp.exp(m - m_new)
        p = jnp.exp(s - m_new)
        l = l * alpha + jnp.sum(p, axis=0, keepdims=True)
        vt = jnp.concatenate([vt_ref[0, 2 * jj], vt_ref[0, 2 * jj + 1]], axis=1)
        acc_scr[...] = acc_scr[...] * alpha + jnp.dot(vt, p.astype(BF16), preferred_element_type=F32)
        return m_new, l

    init = (jnp.full((1, H * tq), NEG, F32), jnp.zeros((1, H * tq), F32))
    m, l = lax.fori_loop(0, npair, attn_pair, init)
    out = acc_scr[...] * (1.0 / l)
    for h in range(H):
        o_ref[0, :, h * HEAD_DIM:(h + 1) * HEAD_DIM] = out[:, h * tq:(h + 1) * tq].T.astype(o_ref.dtype)


def _dsa_attention(qd, kd, qi, kdup, vt, wit, k_sel):
    B, L, _ = qd.shape
    tq = _ATT_TK
    nq = L // tq
    nvs = tuple(sorted({-(-nq * f // 4) for f in (1, 2, 3, 4)}))
    return pl.pallas_call(
        functools.partial(_dsa_body, tq=tq, k_sel=k_sel, nvs=nvs),
        grid=(B, nq),
        in_specs=[pl.BlockSpec((1, tq, D_DSA), lambda b, i: (b, i, 0)),
                  pl.BlockSpec((1, tq, IDX_HEADS * IDX_DIM), lambda b, i: (b, i, 0)),
                  pl.BlockSpec((1, IDX_HEADS, tq), lambda b, i: (b * nq + i, 0, 0)),
                  pl.BlockSpec((1, L, HEAD_DIM), lambda b, i: (b, 0, 0)),
                  pl.BlockSpec((1, L, LANES), lambda b, i: (b, 0, 0)),
                  pl.BlockSpec((1, nq, HEAD_DIM, tq), lambda b, i: (b, 0, 0, 0))],
        out_specs=pl.BlockSpec((1, tq, D_DSA), lambda b, i: (b, i, 0)),
        out_shape=jax.ShapeDtypeStruct((B, L, D_DSA), BF16),
        scratch_shapes=[pltpu.VMEM((nq, tq, tq), I32),
                        pltpu.VMEM((IDX_HEADS * tq, LANES), BF16),
                        pltpu.VMEM((HEAD_DIM, DSA_HEADS * tq), F32),
                        pltpu.VMEM((8, tq), I32)],
        compiler_params=_cparams(("arbitrary", "arbitrary")),
        name="dsa_attention",
    )(qd, qi, wit, kd, kdup, vt)


def _outproj_body(x_ref, ya_ref, yb_ref, yc_ref, w_ref, g_ref, wrh_ref, wrl_ref, br_ref,
                  x2_ref, lg_ref):
    acc = x_ref[...]
    acc = acc + jnp.dot(ya_ref[...], w_ref[0:D_CONV, :], preferred_element_type=F32)
    acc = acc + jnp.dot(yb_ref[...], w_ref[D_CONV:D_CONV + D_FOX, :], preferred_element_type=F32)
    acc = acc + jnp.dot(yc_ref[...], w_ref[D_CONV + D_FOX:, :], preferred_element_type=F32)
    x2_ref[...] = acc
    h = acc * lax.rsqrt(jnp.mean(acc * acc, axis=-1, keepdims=True) + EPS) * g_ref[...]
    hh = h.astype(BF16)
    hl = (h - hh.astype(F32)).astype(BF16)
    lg = jnp.dot(hh, wrh_ref[...], preferred_element_type=F32)
    lg = lg + jnp.dot(hh, wrl_ref[...], preferred_element_type=F32)
    lg = lg + jnp.dot(hl, wrh_ref[...], preferred_element_type=F32)
    lg_ref[...] = lg + br_ref[...]


def _outproj(x2d, ya, yb, yc, w_out, g_ffn, wr_hi, wr_lo, br, tm=512):
    T, D = x2d.shape
    tm = min(tm, T)
    row = lambda n: pl.BlockSpec((tm, n), lambda i: (i, 0))
    const = lambda a: pl.BlockSpec(a.shape, lambda i: (0, 0))
    return pl.pallas_call(
        _outproj_body,
        grid=(T // tm,),
        in_specs=[row(D), row(D_CONV), row(D_FOX), row(D_DSA), const(w_out), const(g_ffn),
                  const(wr_hi), const(wr_lo), const(br)],
        out_specs=[row(D), row(LANES)],
        out_shape=[jax.ShapeDtypeStruct((T, D), F32), jax.ShapeDtypeStruct((T, LANES), F32)],
        compiler_params=_cparams(("parallel",)),
        name="outproj",
    )(x2d, ya, yb, yc, w_out, g_ffn, wr_hi, wr_lo, br)


def _route_body(lg_ref, info_ref, gate_ref, cnt_ref):
    i = pl.program_id(0)
    lg = lg_ref[...]
    tm = lg.shape[0]
    lane = lax.broadcasted_iota(I32, lg.shape, 1)
    ninf = -jnp.inf

    lane_f = lane.astype(F32)

    def first_lane(mask):
        return jnp.min(jnp.where(mask, lane_f, float(LANES)), axis=-1, keepdims=True).astype(I32)

    gl = jnp.where(lane < N_GROUPS, lg, ninf)
    gexp = jnp.exp(gl - jnp.max(gl, axis=-1, keepdims=True))
    gprob = gexp / jnp.sum(gexp, axis=-1, keepdims=True)
    p_g = jnp.max(gprob, axis=-1, keepdims=True)
    g_idx = first_lane(gprob == p_g)

    e_lo = N_GROUPS + EXPERTS_PER_GROUP * g_idx
    emask = jnp.logical_and(lane >= e_lo, lane < e_lo + EXPERTS_PER_GROUP)
    el = jnp.where(emask, lg, ninf)
    eexp = jnp.exp(el - jnp.max(el, axis=-1, keepdims=True))
    eprob = jnp.where(emask, eexp / jnp.sum(eexp, axis=-1, keepdims=True), -1.0)
    p1 = jnp.max(eprob, axis=-1, keepdims=True)
    l1 = first_lane(eprob == p1)
    eprob2 = jnp.where(lane == l1, -1.0, eprob)
    p2 = jnp.max(eprob2, axis=-1, keepdims=True)
    l2 = first_lane(eprob2 == p2)
    den = p1 + p2
    gate1 = p_g * p1 / den
    gate2 = p_g * p2 / den
    eid1 = l1 - N_GROUPS
    eid2 = l2 - N_GROUPS

    @pl.when(i == 0)
    def _():
        cnt_ref[...] = jnp.zeros(cnt_ref.shape, F32)

    hot1 = lane == eid1
    hot2 = lane == eid2
    onehot = jnp.where(jnp.logical_or(hot1, hot2), 1.0, 0.0)
    r = lax.broadcasted_iota(I32, (tm, tm), 0)
    c = lax.broadcasted_iota(I32, (tm, tm), 1)
    tri = jnp.where(c < r, 1.0, 0.0).astype(BF16)
    before = jnp.dot(tri, onehot.astype(BF16), preferred_element_type=F32) + cnt_ref[0:1, :]
    rank1 = jnp.sum(jnp.where(hot1, before, 0.0), axis=-1, keepdims=True).astype(I32)
    rank2 = jnp.sum(jnp.where(hot2, before, 0.0), axis=-1, keepdims=True).astype(I32)
    cnt_ref[...] = cnt_ref[...] + jnp.sum(onehot, axis=0, keepdims=True)

    info_ref[...] = jnp.where(lane == 0, eid1, jnp.where(lane == 1, eid2,
                              jnp.where(lane == 2, rank1, jnp.where(lane == 3, rank2, 0))))
    gate_ref[...] = jnp.where(lane == 0, gate1, jnp.where(lane == 1, gate2, 0.0))


def _route(logits, tm=512):
    T = logits.shape[0]
    tm = min(tm, T)
    row = pl.BlockSpec((tm, LANES), lambda i: (i, 0))
    return pl.pallas_call(
        _route_body,
        grid=(T // tm,),
        in_specs=[row],
        out_specs=[row, row, pl.BlockSpec((8, LANES), lambda i: (0, 0))],
        out_shape=[jax.ShapeDtypeStruct((T, LANES), I32), jax.ShapeDtypeStruct((T, LANES), F32),
                   jax.ShapeDtypeStruct((8, LANES), F32)],
        compiler_params=_cparams(("arbitrary",)),
        name="route",
    )(logits)


def _row_copy(src, s, dst, d, sem):
    return pltpu.make_async_copy(src.at[pl.ds(s, 1)], dst.at[pl.ds(d, 1)], sem)


def _dispatch_body(dest_ref, x_ref, xs_hbm, sem, *, td):
    def issue(t, carry):
        _row_copy(x_ref, t, xs_hbm, dest_ref[0, 0, t], sem).start()
        _row_copy(x_ref, t, xs_hbm, dest_ref[0, 1, t], sem).start()
        return carry

    def drain(t, carry):
        _row_copy(x_ref, 0, xs_hbm, 0, sem).wait()
        _row_copy(x_ref, 0, xs_hbm, 0, sem).wait()
        return carry

    lax.fori_loop(0, td, issue, 0)
    lax.fori_loop(0, td, drain, 0)


def _dispatch(x2, dest3, td):
    T, D = x2.shape
    return pl.pallas_call(
        functools.partial(_dispatch_body, td=td),
        grid=(T // td,),
        in_specs=[pl.BlockSpec((1, 2, td), lambda i: (i, 0, 0), memory_space=pltpu.SMEM),
                  pl.BlockSpec((td, D), lambda i: (i, 0))],
        out_specs=pl.BlockSpec(memory_space=pl.ANY),
        out_shape=jax.ShapeDtypeStruct((2 * T, D), F32),
        scratch_shapes=[pltpu.SemaphoreType.DMA(())],
        compiler_params=_cparams(("arbitrary",)),
        name="dispatch",
    )(dest3, x2)


def _expert_body(blk_ref, e_ref, lo_ref, hi_ref, first_ref, xs_ref, g_ref, wg_ref, wu_ref, wd_ref,
                 o_ref):
    t = pl.program_id(0)
    lo = lo_ref[t]
    hi = hi_ref[t]

    @pl.when(first_ref[t] == 1)
    def _():
        o_ref[...] = jnp.zeros(o_ref.shape, F32)

    @pl.when(hi > lo)
    def _():
        x = xs_ref[...]
        h = (x * lax.rsqrt(jnp.mean(x * x, axis=-1, keepdims=True) + EPS) * g_ref[...]).astype(BF16)
        g = jnp.dot(h, wg_ref[0], preferred_element_type=F32)
        u = jnp.dot(h, wu_ref[0], preferred_element_type=F32)
        hdn = (g * jax.nn.sigmoid(g) * u).astype(BF16)
        y = jnp.dot(hdn, wd_ref[0], preferred_element_type=F32)
        row = lax.broadcasted_iota(I32, (y.shape[0], 1), 0)
        keep = jnp.logical_and(row >= lo, row < hi)
        o_ref[...] = o_ref[...] + jnp.where(keep, y, 0.0)


def _expert_ffn(xs, g_ffn, wg, wu, wd, items, bm):
    A, D = xs.shape
    blk, eid, lo, hi, first = items
    ni = blk.shape[0]
    grid_spec = pltpu.PrefetchScalarGridSpec(
        num_scalar_prefetch=5,
        grid=(ni,),
        in_specs=[pl.BlockSpec((bm, D), lambda t, b, e, lo, hi, f: (b[t], 0)),
                  pl.BlockSpec((1, D), lambda t, b, e, lo, hi, f: (0, 0)),
                  pl.BlockSpec((1, D, D_EXPERT), lambda t, b, e, lo, hi, f: (e[t], 0, 0)),
                  pl.BlockSpec((1, D, D_EXPERT), lambda t, b, e, lo, hi, f: (e[t], 0, 0)),
                  pl.BlockSpec((1, D_EXPERT, D), lambda t, b, e, lo, hi, f: (e[t], 0, 0))],
        out_specs=pl.BlockSpec((bm, D), lambda t, b, e, lo, hi, f: (b[t], 0)),
    )
    return pl.pallas_call(
        _expert_body,
        grid_spec=grid_spec,
        out_shape=jax.ShapeDtypeStruct((A, D), F32),
        compiler_params=_cparams(("arbitrary",)),
        name="expert_ffn",
    )(blk, eid, lo, hi, first, xs, g_ffn, wg, wu, wd)


def _combine_body(dest_ref, x_ref, gate_ref, gf_ref, ys_hbm, o_ref, buf, sem, *, tc, final):
    def issue(t, carry):
        pltpu.make_async_copy(ys_hbm.at[pl.ds(dest_ref[0, 0, t], 1)], buf.at[0, pl.ds(t, 1)], sem).start()
        pltpu.make_async_copy(ys_hbm.at[pl.ds(dest_ref[0, 1, t], 1)], buf.at[1, pl.ds(t, 1)], sem).start()
        return carry

    def drain(t, carry):
        pltpu.make_async_copy(ys_hbm.at[pl.ds(0, 1)], buf.at[0, pl.ds(0, 1)], sem).wait()
        pltpu.make_async_copy(ys_hbm.at[pl.ds(0, 1)], buf.at[0, pl.ds(0, 1)], sem).wait()
        return carry

    lax.fori_loop(0, tc, issue, 0)
    lax.fori_loop(0, tc, drain, 0)
    g = gate_ref[...]
    out = x_ref[...] + g[:, 0:1] * buf[0] + g[:, 1:2] * buf[1]
    if final:
        out = out * lax.rsqrt(jnp.mean(out * out, axis=-1, keepdims=True) + EPS) * gf_ref[...]
    o_ref[...] = out


def _combine(x2, dest3, gates, ys, g_final, tc, final):
    T, D = x2.shape
    return pl.pallas_call(
        functools.partial(_combine_body, tc=tc, final=final),
        grid=(T // tc,),
        in_specs=[pl.BlockSpec((1, 2, tc), lambda i: (i, 0, 0), memory_space=pltpu.SMEM),
                  pl.BlockSpec((tc, D), lambda i: (i, 0)),
                  pl.BlockSpec((tc, LANES), lambda i: (i, 0)),
                  pl.BlockSpec((1, D), lambda i: (0, 0)),
                  pl.BlockSpec(memory_space=pl.ANY)],
        out_specs=pl.BlockSpec((tc, D), lambda i: (i, 0)),
        out_shape=jax.ShapeDtypeStruct((T, D), F32),
        scratch_shapes=[pltpu.VMEM((2, tc, D), F32), pltpu.SemaphoreType.DMA(())],
        compiler_params=_cparams(("arbitrary",)),
        name="combine",
    )(dest3, x2, gates, g_final, ys)


def _rope_tables(L):
    pos = jnp.arange(L, dtype=F32)

    def cs(dim):
        inv = ROPE_THETA ** (-jnp.arange(0, dim, 2, dtype=F32) / dim)
        ang = pos[:, None] * inv[None, :]
        return jnp.cos(ang), jnp.sin(ang)

    ch, sh = cs(HEAD_DIM)
    ci, si = cs(IDX_DIM)
    zi = jnp.zeros_like(si)
    scale = HEAD_DIM ** -0.5
    cos128 = jnp.concatenate([ch, ch], axis=1)
    sin128 = jnp.concatenate([-sh, sh], axis=1)
    tabs = [
        cos128 * scale, sin128 * scale,
        cos128, sin128,
        jnp.concatenate([ci, ci, ci, ci], axis=1),
        jnp.concatenate([-si, zi, -si, zi], axis=1),
        jnp.concatenate([zi, si, zi, si], axis=1),
    ]
    return jnp.stack(tabs, axis=0)


def _work_items(counts, A, bm):
    starts = jnp.cumsum(counts) - counts
    ends = starts + counts
    nblk = A // bm
    bstart = jnp.arange(nblk, dtype=I32) * bm
    estart = starts[1:].astype(I32)
    bpos = jnp.arange(nblk, dtype=I32) + jnp.sum(estart[None, :] < bstart[:, None], axis=1).astype(I32)
    epos = jnp.arange(N_EXPERTS - 1, dtype=I32) + jnp.minimum(estart // bm + 1, nblk)
    bounds = jnp.zeros((nblk + N_EXPERTS - 1,), I32).at[bpos].set(bstart).at[epos].set(estart)
    nxt = jnp.concatenate([bounds[1:], jnp.array([A], I32)])
    blk = jnp.minimum(bounds // bm, nblk - 1)
    eid = jnp.minimum(jnp.sum(ends[None, :] <= bounds[:, None], axis=1), N_EXPERTS - 1).astype(I32)
    lo = bounds - blk * bm
    hi = nxt - blk * bm
    first = jnp.concatenate([jnp.ones((1,), I32), (blk[1:] != blk[:-1]).astype(I32)])
    return blk.astype(I32), eid, lo.astype(I32), hi.astype(I32), first


def _pick(n, prefs):
    for p in prefs:
        if n % p == 0:
            return p
    return n


def kernel(x, norm_mix, w_in, conv_w, conv_b, conv_ln_g, conv_ln_b, fox_fb, w_out, norm_ffn,
           w_router_group, b_router_group, w_router_expert, b_router_expert, w_gate, w_up, w_down,
           norm_final):
    B, L, D = x.shape
    T = B * L
    depth = w_in.shape[0]
    k_sel = min(TOPK_MAX, L // 4)
    tabs = _rope_tables(L)
    fox_tq = _pick(L, (256, 128))
    bm = _pick(2 * T, (512, 256))
    td = _pick(T, (256, 128))

    tk = _ATT_TK
    nk = L // tk
    fox_tk = min(_FOX_TK, fox_tq)
    segs_a = [("plain", 0, 2 * D_CONV, 0, 0, (), 1.0),
              ("plain", _O_QF, D_FOX, 1, 0, (), HEAD_DIM ** -0.5),
              ("plain", _O_KF, D_FOX, 1, D_FOX, (), 1.0)]
    outs_a = [(2 * D_CONV, BF16), (2 * D_FOX, BF16)]
    c_kd = D_DSA
    c_qi = c_kd + HEAD_DIM
    c_kk = c_qi + IDX_HEADS * IDX_DIM
    c_sm = c_kk + LANES
    segs_b = [("rope128", 0, D_DSA, 0, 0, (0, 1), 1.0),
              ("rope128", c_kd, HEAD_DIM, 1, 0, (2, 3), 1.0),
              ("rope64", c_qi, IDX_HEADS * IDX_DIM, 2, 0, (4, 5, 6), 1.0),
              ("rope64", c_kk, LANES, 3, 0, (4, 5, 6), 1.0),
              ("plain", c_sm, LANES, 4, 0, (), 1.0)]
    outs_b = [(D_DSA, BF16), (HEAD_DIM, BF16), (IDX_HEADS * IDX_DIM, BF16), (LANES, BF16), (LANES, F32)]

    xf = x.reshape(T, D)
    for l in range(depth):
        wl = w_in[l]
        w_a = wl[:, :_O_VF].astype(BF16)
        w_vf_t = wl[:, _O_VF:_O_FL].T.astype(BF16)
        w_ki = wl[:, _O_KI:_O_WI]
        pad = jnp.zeros((D, LANES - IDX_DIM - IDX_HEADS - FOX_HEADS), F32)
        w_b = jnp.concatenate([wl[:, _O_QD:_O_VD], wl[:, _O_QI:_O_KI], w_ki, w_ki,
                               w_ki, wl[:, _O_WI:_O_END], wl[:, _O_FL:_O_QD], pad], axis=1).astype(BF16)
        w_vd_t = wl[:, _O_VD:_O_QI].T.astype(BF16)
        w_wi_t = wl[:, _O_WI:_O_END].T.astype(BF16)
        fb_row = jnp.zeros((1, LANES), F32).at[0, _S_FL:_S_FL + FOX_HEADS].set(fox_fb[l])

        u, fqk, fvt = _proj(xf, norm_mix[l], w_a, tabs, segs_a, outs_a, L,
                            t_groups=[(w_vf_t, fox_tk, BF16)])
        qd, kd, qi, kdup, small, dvt, wit = _proj(xf, norm_mix[l], w_b, tabs, segs_b, outs_b, L,
                                                 t_groups=[(w_vd_t, tk, BF16), (w_wi_t, tk, F32)])

        c_tm = _fox_cumsum(small.reshape(B, L, LANES), fb_row)
        ya = _conv_module(u.reshape(B, L, -1), conv_w[l], conv_b[l], conv_ln_g[l], conv_ln_b[l])
        yb = _fox_attention(fqk.reshape(B, L, -1), fvt.reshape(B, L // fox_tk, D_FOX, fox_tk), c_tm,
                            tq=fox_tq)
        yc = _dsa_attention(qd.reshape(B, L, -1), kd.reshape(B, L, -1), qi.reshape(B, L, -1),
                            kdup.reshape(B, L, -1), dvt.reshape(B, nk, HEAD_DIM, tk), wit, k_sel)

        wr = jnp.concatenate([w_router_group[l], w_router_expert[l],
                              jnp.zeros((D, LANES - N_GROUPS - N_EXPERTS), F32)], axis=1)
        wr_hi = wr.astype(BF16)
        wr_lo = (wr - wr_hi.astype(F32)).astype(BF16)
        br = jnp.concatenate([b_router_group[l], b_router_expert[l],
                              jnp.zeros((LANES - N_GROUPS - N_EXPERTS,), F32)]).reshape(1, LANES)
        g_ffn = norm_ffn[l].reshape(1, D)
        x2, logits = _outproj(xf, ya.reshape(T, -1), yb.reshape(T, -1), yc.reshape(T, -1),
                              w_out[l].astype(BF16), g_ffn, wr_hi, wr_lo, br)

        info, gates, cnt = _route(logits)
        counts = cnt[0, :N_EXPERTS].astype(I32)
        starts = jnp.cumsum(counts) - counts
        dest = starts[info[:, 0:2]] + info[:, 2:4]
        dest3 = dest.reshape(T // td, td, 2).transpose(0, 2, 1)

        xs = _dispatch(x2, dest3, td)
        items = _work_items(counts, 2 * T, bm)
        ys = _expert_ffn(xs, g_ffn, w_gate[l].astype(BF16), w_up[l].astype(BF16),
                         w_down[l].astype(BF16), items, bm)
        xf = _combine(x2, dest3, gates, ys, norm_final.reshape(1, D), td, final=(l == depth - 1))
    return xf.reshape(B, L, D)
```

```python
import functools

import jax
import jax.numpy as jnp
import numpy as np
from jax import lax
from jax.experimental import pallas as pl
from jax.experimental.pallas import tpu as pltpu

F32 = jnp.float32
BF16 = jnp.bfloat16
I32 = jnp.int32

D_MODEL = 2048
HEAD_DIM = 128
D_CONV = 512
CONV_WIDTH = 31
FOX_HEADS = 6
DSA_HEADS = 6
D_FOX = FOX_HEADS * HEAD_DIM
D_DSA = DSA_HEADS * HEAD_DIM
IDX_HEADS = 16
IDX_DIM = 64
TOPK_MAX = 256
ROPE_THETA = 10000.0
CHUNK = 64
N_GROUPS = 4
EXPERTS_PER_GROUP = 8
N_EXPERTS = N_GROUPS * EXPERTS_PER_GROUP
D_EXPERT = 512
EPS = 1e-6

LANES = 128
NEG = -1e30
INT_MIN = -(2 ** 31)
VMEM_LIMIT = 56 * 1024 * 1024

_O_U = 0
_O_QF = _O_U + 2 * D_CONV
_O_KF = _O_QF + D_FOX
_O_VF = _O_KF + D_FOX
_O_FL = _O_VF + D_FOX
_O_QD = _O_FL + FOX_HEADS
_O_KD = _O_QD + D_DSA
_O_VD = _O_KD + HEAD_DIM
_O_QI = _O_VD + HEAD_DIM
_O_KI = _O_QI + IDX_HEADS * IDX_DIM
_O_WI = _O_KI + IDX_DIM
_O_END = _O_WI + IDX_HEADS

_S_KI = 0
_S_WI = IDX_DIM
_S_FL = IDX_DIM + IDX_HEADS


def _cparams(sem):
    return pltpu.CompilerParams(dimension_semantics=sem, vmem_limit_bytes=VMEM_LIMIT)


def _proj_body(x_ref, g_ref, w_ref, tab_ref, *rest, segs, n_t, n_out):
    wts, outs, outs_t = rest[:n_t], rest[n_t:n_t + n_out], rest[n_t + n_out:]
    x = x_ref[...]
    inv = lax.rsqrt(jnp.mean(x * x, axis=-1, keepdims=True) + EPS)
    h = (x * inv * g_ref[...]).astype(BF16)
    for (kind, w0, width, oi, o0, tabs, scale) in segs:
        z = jnp.dot(h, w_ref[:, w0:w0 + width], preferred_element_type=F32)
        for c in range(width // LANES):
            zc = z[:, c * LANES:(c + 1) * LANES]
            if kind == "plain":
                if scale != 1.0:
                    zc = zc * scale
            elif kind == "rope128":
                zc = zc * tab_ref[tabs[0]] + pltpu.roll(zc, 64, 1) * tab_ref[tabs[1]]
            else:
                zc = (zc * tab_ref[tabs[0]] + pltpu.roll(zc, 96, 1) * tab_ref[tabs[1]]
                      + pltpu.roll(zc, 32, 1) * tab_ref[tabs[2]])
            outs[oi][:, o0 + c * LANES:o0 + (c + 1) * LANES] = zc.astype(outs[oi].dtype)
    for wt_ref, o_ref in zip(wts, outs_t):
        zt = lax.dot_general(wt_ref[...], h, (((1,), (1,)), ((), ())), preferred_element_type=F32)
        tw = o_ref.shape[2]
        for c in range(o_ref.shape[0]):
            o_ref[c] = zt[:, c * tw:(c + 1) * tw].astype(o_ref.dtype)


def _proj(x2d, g, w, tabs, segs, out_defs, seq_len, t_groups=(), tm=512):
    T, D = x2d.shape
    tm = min(tm, seq_len)
    nt = T // tm
    nl = seq_len // tm
    ntab = tabs.shape[0]
    in_specs = [
        pl.BlockSpec((tm, D), lambda i: (i, 0)),
        pl.BlockSpec((1, D), lambda i: (0, 0)),
        pl.BlockSpec(w.shape, lambda i: (0, 0)),
        pl.BlockSpec((ntab, tm, LANES), lambda i: (0, i % nl, 0)),
    ]
    args = [x2d, g.reshape(1, D), w, tabs]
    out_shape = [jax.ShapeDtypeStruct((T, n), dt) for (n, dt) in out_defs]
    out_specs = [pl.BlockSpec((tm, n), lambda i: (i, 0)) for (n, dt) in out_defs]
    for (wt, tw, dt) in t_groups:
        in_specs.append(pl.BlockSpec(wt.shape, lambda i: (0, 0)))
        args.append(wt)
    for (wt, tw, dt) in t_groups:
        out_shape.append(jax.ShapeDtypeStruct((T // tw, wt.shape[0], tw), dt))
        out_specs.append(pl.BlockSpec((tm // tw, wt.shape[0], tw), lambda i: (i, 0, 0)))
    return pl.pallas_call(
        functools.partial(_proj_body, segs=tuple(segs), n_t=len(t_groups), n_out=len(out_defs)),
        grid=(nt,),
        in_specs=in_specs,
        out_specs=out_specs,
        out_shape=out_shape,
        compiler_params=_cparams(("parallel",)),
        name="proj",
    )(*args)


def _cumsum_body(fl_ref, fb_ref, o_ref):
    z = fl_ref[0] + fb_ref[...]
    x = jnp.minimum(z, 0.0) - jnp.log(1.0 + jnp.exp(-jnp.abs(z)))
    L = x.shape[0]
    row = lax.broadcasted_iota(I32, x.shape, 0)
    s = 1
    while s < L:
        x = x + jnp.where(row >= s, pltpu.roll(x, s, 0), 0.0)
        s *= 2
    o_ref[0] = x


def _fox_cumsum(small, fb_row):
    B, L, _ = small.shape
    return pl.pallas_call(
        _cumsum_body,
        grid=(B,),
        in_specs=[pl.BlockSpec((1, L, LANES), lambda b: (b, 0, 0)),
                  pl.BlockSpec((1, LANES), lambda b: (0, 0))],
        out_specs=pl.BlockSpec((1, L, LANES), lambda b: (b, 0, 0)),
        out_shape=jax.ShapeDtypeStruct((B, L, LANES), F32),
        compiler_params=_cparams(("parallel",)),
        name="fox_cumsum",
    )(small, fb_row)


_CONV_HALO = 32


def _conv_body(u_ref, w_ref, cb_ref, g_ref, b_ref, o_ref, a_scr, *, rc):
    L = u_ref.shape[1]
    a_scr[0:_CONV_HALO, :] = jnp.zeros((_CONV_HALO, D_CONV), F32)
    u1 = u_ref[0, :, 0:D_CONV].astype(F32)
    u2 = u_ref[0, :, D_CONV:2 * D_CONV].astype(F32)
    a_scr[_CONV_HALO:_CONV_HALO + L, :] = u1 * jax.nn.sigmoid(u2)
    win = rc + _CONV_HALO

    def chunk(r, carry):
        base = pl.multiple_of(r * rc, 8)
        wnd = a_scr[pl.ds(base, win), :]
        acc = jnp.zeros((rc, D_CONV), F32) + cb_ref[...]
        for j in range(CONV_WIDTH):
            off = _CONV_HALO - (CONV_WIDTH - 1) + j
            sh = pltpu.roll(wnd, win - off, 0)[0:rc]
            acc = acc + sh * w_ref[j:j + 1, :]
        mu = jnp.mean(acc, axis=-1, keepdims=True)
        d = acc - mu
        var = jnp.mean(d * d, axis=-1, keepdims=True)
        y = d * lax.rsqrt(var + EPS) * g_ref[...] + b_ref[...]
        o_ref[0, pl.ds(base, rc), :] = (y * jax.nn.sigmoid(y)).astype(o_ref.dtype)
        return carry

    lax.fori_loop(0, L // rc, chunk, 0)


def _conv_module(u, conv_w, conv_b, ln_g, ln_b, rc=256):
    B, L, _ = u.shape
    rc = min(rc, L)
    return pl.pallas_call(
        functools.partial(_conv_body, rc=rc),
        grid=(B,),
        in_specs=[pl.BlockSpec((1, L, 2 * D_CONV), lambda b: (b, 0, 0)),
                  pl.BlockSpec((CONV_WIDTH, D_CONV), lambda b: (0, 0)),
                  pl.BlockSpec((1, D_CONV), lambda b: (0, 0)),
                  pl.BlockSpec((1, D_CONV), lambda b: (0, 0)),
                  pl.BlockSpec((1, D_CONV), lambda b: (0, 0))],
        out_specs=pl.BlockSpec((1, L, D_CONV), lambda b: (b, 0, 0)),
        out_shape=jax.ShapeDtypeStruct((B, L, D_CONV), BF16),
        scratch_shapes=[pltpu.VMEM((L + _CONV_HALO, D_CONV), F32)],
        compiler_params=_cparams(("parallel",)),
        name="conv_module",
    )(u, conv_w, conv_b.reshape(1, -1), ln_g.reshape(1, -1), ln_b.reshape(1, -1))


_ATT_TK = 128
_FOX_TK = 128


def _fox_body(q_ref, k_ref, vt_ref, c_ref, o_ref, cb_scr, acc_scr, *, tq, tk):
    qi = pl.program_id(1)
    L = k_ref.shape[1]
    H = FOX_HEADS
    nsub = tq // tk

    @pl.when(qi == 0)
    def _():
        def fill(r, carry):
            off = pl.multiple_of(r * tk, tk)
            c = c_ref[0, pl.ds(off, tk), :]
            for h in range(H):
                cb_scr[h, pl.ds(off, tk), :] = jnp.broadcast_to(
                    c[:, _S_FL + h:_S_FL + h + 1], (tk, LANES))
            return carry

        lax.fori_loop(0, L // tk, fill, 0)

    q = [q_ref[0, :, h * HEAD_DIM:(h + 1) * HEAD_DIM] for h in range(H)]
    acc_scr[...] = jnp.zeros(acc_scr.shape, F32)

    def tile(j, carry, diag):
        ms, ls = carry
        off = pl.multiple_of(j * tk, tk)
        new_m, new_l, alphas, ps = [], [], [], []
        ss = [lax.dot_general(k_ref[0, pl.ds(off, tk), h * HEAD_DIM:(h + 1) * HEAD_DIM], q[h],
                              (((1,), (1,)), ((), ())), preferred_element_type=F32) for h in range(H)]
        for h in range(H):
            cb = cb_scr[h, pl.ds(off, tk), :]
            s = ss[h] - jnp.concatenate([cb] * (tq // LANES), axis=1)
            if diag:
                kk = off + lax.broadcasted_iota(I32, s.shape, 0)
                qq = qi * tq + lax.broadcasted_iota(I32, s.shape, 1)
                s = jnp.where(kk <= qq, s, NEG)
            m_new = jnp.maximum(ms[h], jnp.max(s, axis=0, keepdims=True))
            alpha = jnp.exp(ms[h] - m_new)
            p = jnp.exp(s - m_new)
            new_l.append(ls[h] * alpha + jnp.sum(p, axis=0, keepdims=True))
            new_m.append(m_new)
            alphas.append(alpha)
            ps.append(p.astype(BF16))
        pvs = [jnp.dot(vt_ref[0, j, h * HEAD_DIM:(h + 1) * HEAD_DIM, :], ps[h],
                       preferred_element_type=F32) for h in range(H)]
        for h in range(H):
            acc_scr[h] = acc_scr[h] * alphas[h] + pvs[h]
        return tuple(new_m), tuple(new_l)

    init = (tuple(jnp.full((1, tq), NEG, F32) for _ in range(H)),
            tuple(jnp.zeros((1, tq), F32) for _ in range(H)))
    carry = lax.fori_loop(0, qi * nsub, functools.partial(tile, diag=False), init)
    for d in range(nsub):
        carry = tile(qi * nsub + d, carry, True)
    ms, ls = carry
    for h in range(H):
        out = acc_scr[h] * (1.0 / ls[h])
        o_ref[0, :, h * HEAD_DIM:(h + 1) * HEAD_DIM] = out.T.astype(o_ref.dtype)


def _fox_attention(qk, vt, c_tm, tq=256):
    B, L, _ = qk.shape
    tq = min(tq, L)
    nq = L // tq
    nk, tk = vt.shape[1], vt.shape[3]
    return pl.pallas_call(
        functools.partial(_fox_body, tq=tq, tk=tk),
        grid=(B, nq),
        in_specs=[pl.BlockSpec((1, tq, D_FOX), lambda b, i: (b, i, 0)),
                  pl.BlockSpec((1, L, D_FOX), lambda b, i: (b, 0, 1)),
                  pl.BlockSpec((1, nk, D_FOX, tk), lambda b, i: (b, 0, 0, 0)),
                  pl.BlockSpec((1, L, LANES), lambda b, i: (b, 0, 0))],
        out_specs=pl.BlockSpec((1, tq, D_FOX), lambda b, i: (b, i, 0)),
        out_shape=jax.ShapeDtypeStruct((B, L, D_FOX), BF16),
        scratch_shapes=[pltpu.VMEM((FOX_HEADS, L, LANES), F32),
                        pltpu.VMEM((FOX_HEADS, HEAD_DIM, tq), F32)],
        compiler_params=_cparams(("arbitrary", "arbitrary")),
        name="fox_attention",
    )(qk, qk, vt, c_tm)


def _dsa_body(qd_ref, qi_ref, wt_ref, k_ref, kdup_ref, vt_ref, o_ref,
              key_scr, rhs_scr, acc_scr, thr_scr, *, tq, k_sel, nvs):
    qb = pl.program_id(1)
    H = DSA_HEADS
    half = IDX_DIM

    lane = lax.broadcasted_iota(I32, (tq, LANES), 1)
    for p in range(IDX_HEADS // 2):
        qp = qi_ref[0, :, p * LANES:(p + 1) * LANES].astype(F32)
        rhs_scr[(2 * p) * tq:(2 * p + 1) * tq, :] = jnp.where(lane < half, qp, 0.0).astype(BF16)
        rhs_scr[(2 * p + 1) * tq:(2 * p + 2) * tq, :] = jnp.where(lane >= half, qp, 0.0).astype(BF16)
    wq = wt_ref[0] * ((IDX_DIM ** -0.5) * (IDX_HEADS ** -0.5))

    key_scr[...] = jnp.full(key_scr.shape, INT_MIN, I32)
    rowk = lax.broadcasted_iota(I32, (tq, tq), 0)
    colq = lax.broadcasted_iota(I32, (tq, tq), 1)
    admissible_diag = (rowk // CHUNK) <= (colq // CHUNK)

    npair = (qb + 2) // 2

    def score_pair(jj, carry):
        for g in range(2):
            j = 2 * jj + g
            off = pl.multiple_of(j * tq, tq)
            kd = kdup_ref[0, pl.ds(off, tq), :]
            s = jnp.zeros((tq, tq), F32)
            for p in range(IDX_HEADS // 2):
                r = lax.dot_general(kd, rhs_scr[(2 * p) * tq:(2 * p + 2) * tq, :],
                                    (((1,), (1,)), ((), ())), preferred_element_type=F32)
                s = s + jnp.maximum(r[:, :tq], 0.0) * wq[2 * p:2 * p + 1, :]
                s = s + jnp.maximum(r[:, tq:], 0.0) * wq[2 * p + 1:2 * p + 2, :]
            bits = pltpu.bitcast(s, I32)
            key = jnp.where(bits < 0, bits ^ jnp.int32(0x7FFFFFFF), bits)
            adm = jnp.logical_or(j < qb, jnp.logical_and(j == qb, admissible_diag))
            key_scr[j] = jnp.where(adm, key, INT_MIN)
        return carry

    lax.fori_loop(0, npair, score_pair, 0)

    def select(nv):
        def bit_step(i, t):
            cand = t + (jnp.int32(1) << (31 - i))
            cnt = jnp.sum(jnp.where(key_scr[0:nv] >= cand[None], 1.0, 0.0), axis=0)
            cnt = jnp.sum(cnt, axis=0, keepdims=True)
            return jnp.where(cnt >= float(k_sel), cand, t)

        t = lax.fori_loop(0, 32, bit_step, jnp.full((1, tq), INT_MIN, I32))
        t = jnp.maximum(t, INT_MIN + 1)
        thr_scr[...] = jnp.broadcast_to(t, thr_scr.shape)

    for idx, nv in enumerate(nvs):
        lo = nvs[idx - 1] if idx else 0

        @pl.when(jnp.logical_and(qb + 1 > lo, qb + 1 <= nv))
        def _(nv=nv):
            select(nv)

    thr = thr_scr[0:1, :]
    qs = jnp.concatenate([qd_ref[0, :, h * HEAD_DIM:(h + 1) * HEAD_DIM] for h in range(H)], axis=0)
    acc_scr[...] = jnp.zeros(acc_scr.shape, F32)

    def attn_pair(jj, carry):
        m, l = carry
        off = pl.multiple_of(jj * (2 * tq), 2 * tq)
        k = k_ref[0, pl.ds(off, 2 * tq), :]
        s = lax.dot_general(k, qs, (((1,), (1,)), ((), ())), preferred_element_type=F32)
        sel = key_scr[pl.ds(2 * jj, 2)].reshape(2 * tq, tq) >= thr
        s = jnp.concatenate([jnp.where(sel, s[:, h * tq:(h + 1) * tq], NEG) for h in range(H)], axis=1)
        m_new = jnp.maximum(m, jnp.max(s, axis=0, keepdims=True))
        alpha = jnp.exp(m - m_new)
        p = jnp.exp(s - m_new)
        l = l * alpha + jnp.sum(p, axis=0, keepdims=True)
        vt = jnp.concatenate([vt_ref[0, 2 * jj], vt_ref[0, 2 * jj + 1]], axis=1)
        acc_scr[...] = acc_scr[...] * alpha + jnp.dot(vt, p.astype(BF16), preferred_element_type=F32)
        return m_new, l

    init = (jnp.full((1, H * tq), NEG, F32), jnp.zeros((1, H * tq), F32))
    m, l = lax.fori_loop(0, npair, attn_pair, init)
    out = acc_scr[...] * (1.0 / l)
    for h in range(H):
        o_ref[0, :, h * HEAD_DIM:(h + 1) * HEAD_DIM] = out[:, h * tq:(h + 1) * tq].T.astype(o_ref.dtype)


def _dsa_attention(qd, kd, qi, kdup, vt, wit, k_sel):
    B, L, _ = qd.shape
    tq = _ATT_TK
    nq = L // tq
    nvs = tuple(sorted({-(-nq * f // 4) for f in (1, 2, 3, 4)}))
    return pl.pallas_call(
        functools.partial(_dsa_body, tq=tq, k_sel=k_sel, nvs=nvs),
        grid=(B, nq),
        in_specs=[pl.BlockSpec((1, tq, D_DSA), lambda b, i: (b, i, 0)),
                  pl.BlockSpec((1, tq, IDX_HEADS * IDX_DIM), lambda b, i: (b, i, 0)),
                  pl.BlockSpec((1, IDX_HEADS, tq), lambda b, i: (b * nq + i, 0, 0)),
                  pl.BlockSpec((1, L, HEAD_DIM), lambda b, i: (b, 0, 0)),
                  pl.BlockSpec((1, L, LANES), lambda b, i: (b, 0, 0)),
                  pl.BlockSpec((1, nq, HEAD_DIM, tq), lambda b, i: (b, 0, 0, 0))],
        out_specs=pl.BlockSpec((1, tq, D_DSA), lambda b, i: (b, i, 0)),
        out_shape=jax.ShapeDtypeStruct((B, L, D_DSA), BF16),
        scratch_shapes=[pltpu.VMEM((nq, tq, tq), I32),
                        pltpu.VMEM((IDX_HEADS * tq, LANES), BF16),
                        pltpu.VMEM((HEAD_DIM, DSA_HEADS * tq), F32),
                        pltpu.VMEM((8, tq), I32)],
        compiler_params=_cparams(("arbitrary", "arbitrary")),
        name="dsa_attention",
    )(qd, qi, wit, kd, kdup, vt)


U32 = jnp.uint32
_HALF = D_MODEL // 2


def _pack_halves(a, b):
    ah = pltpu.bitcast(a.astype(BF16).astype(F32), U32)
    bh = pltpu.bitcast(b.astype(BF16).astype(F32), U32)
    return ah | (bh >> 16)


def _unpack_halves(u):
    a = pltpu.bitcast(u & jnp.uint32(0xFFFF0000), F32)
    b = pltpu.bitcast(u << 16, F32)
    return a, b


def _outproj_body(x_ref, ya_ref, yb_ref, yc_ref, w_ref, g_ref, wrh_ref, wrl_ref, br_ref,
                  x2_ref, lg_ref, hp_ref):
    acc = x_ref[...]
    acc = acc + jnp.dot(ya_ref[...], w_ref[0:D_CONV, :], preferred_element_type=F32)
    acc = acc + jnp.dot(yb_ref[...], w_ref[D_CONV:D_CONV + D_FOX, :], preferred_element_type=F32)
    acc = acc + jnp.dot(yc_ref[...], w_ref[D_CONV + D_FOX:, :], preferred_element_type=F32)
    x2_ref[...] = acc
    h = acc * lax.rsqrt(jnp.mean(acc * acc, axis=-1, keepdims=True) + EPS) * g_ref[...]
    hh = h.astype(BF16)
    hl = (h - hh.astype(F32)).astype(BF16)
    lg = jnp.dot(hh, wrh_ref[...], preferred_element_type=F32)
    lg = lg + jnp.dot(hh, wrl_ref[...], preferred_element_type=F32)
    lg = lg + jnp.dot(hl, wrh_ref[...], preferred_element_type=F32)
    lg_ref[...] = lg + br_ref[...]
    hp_ref[...] = _pack_halves(h[:, :_HALF], h[:, _HALF:])


def _outproj(x2d, ya, yb, yc, w_out, g_ffn, wr_hi, wr_lo, br, tm=512):
    T, D = x2d.shape
    tm = min(tm, T)
    row = lambda n: pl.BlockSpec((tm, n), lambda i: (i, 0))
    const = lambda a: pl.BlockSpec(a.shape, lambda i: (0, 0))
    return pl.pallas_call(
        _outproj_body,
        grid=(T // tm,),
        in_specs=[row(D), row(D_CONV), row(D_FOX), row(D_DSA), const(w_out), const(g_ffn),
                  const(wr_hi), const(wr_lo), const(br)],
        out_specs=[row(D), row(LANES), row(_HALF)],
        out_shape=[jax.ShapeDtypeStruct((T, D), F32), jax.ShapeDtypeStruct((T, LANES), F32),
                   jax.ShapeDtypeStruct((T, _HALF), U32)],
        compiler_params=_cparams(("parallel",)),
        name="outproj",
    )(x2d, ya, yb, yc, w_out, g_ffn, wr_hi, wr_lo, br)


def _route_body(lg_ref, info_ref, gate_ref, cnt_ref):
    i = pl.program_id(0)
    lg = lg_ref[...]
    tm = lg.shape[0]
    lane = lax.broadcasted_iota(I32, lg.shape, 1)
    ninf = -jnp.inf

    lane_f = lane.astype(F32)

    def first_lane(mask):
        return jnp.min(jnp.where(mask, lane_f, float(LANES)), axis=-1, keepdims=True).astype(I32)

    gl = jnp.where(lane < N_GROUPS, lg, ninf)
    gexp = jnp.exp(gl - jnp.max(gl, axis=-1, keepdims=True))
    gprob = gexp / jnp.sum(gexp, axis=-1, keepdims=True)
    p_g = jnp.max(gprob, axis=-1, keepdims=True)
    g_idx = first_lane(gprob == p_g)

    e_lo = N_GROUPS + EXPERTS_PER_GROUP * g_idx
    emask = jnp.logical_and(lane >= e_lo, lane < e_lo + EXPERTS_PER_GROUP)
    el = jnp.where(emask, lg, ninf)
    eexp = jnp.exp(el - jnp.max(el, axis=-1, keepdims=True))
    eprob = jnp.where(emask, eexp / jnp.sum(eexp, axis=-1, keepdims=True), -1.0)
    p1 = jnp.max(eprob, axis=-1, keepdims=True)
    l1 = first_lane(eprob == p1)
    eprob2 = jnp.where(lane == l1, -1.0, eprob)
    p2 = jnp.max(eprob2, axis=-1, keepdims=True)
    l2 = first_lane(eprob2 == p2)
    den = p1 + p2
    gate1 = p_g * p1 / den
    gate2 = p_g * p2 / den
    eid1 = l1 - N_GROUPS
    eid2 = l2 - N_GROUPS

    @pl.when(i == 0)
    def _():
        cnt_ref[...] = jnp.zeros(cnt_ref.shape, F32)

    hot1 = lane == eid1
    hot2 = lane == eid2
    onehot = jnp.where(jnp.logical_or(hot1, hot2), 1.0, 0.0)
    r = lax.broadcasted_iota(I32, (tm, tm), 0)
    c = lax.broadcasted_iota(I32, (tm, tm), 1)
    tri = jnp.where(c < r, 1.0, 0.0).astype(BF16)
    before = jnp.dot(tri, onehot.astype(BF16), preferred_element_type=F32) + cnt_ref[0:1, :]
    rank1 = jnp.sum(jnp.where(hot1, before, 0.0), axis=-1, keepdims=True).astype(I32)
    rank2 = jnp.sum(jnp.where(hot2, before, 0.0), axis=-1, keepdims=True).astype(I32)
    cnt_ref[...] = cnt_ref[...] + jnp.sum(onehot, axis=0, keepdims=True)

    info_ref[...] = jnp.where(lane == 0, eid1, jnp.where(lane == 1, eid2,
                              jnp.where(lane == 2, rank1, jnp.where(lane == 3, rank2, 0))))
    gate_ref[...] = jnp.where(lane == 0, gate1, jnp.where(lane == 1, gate2, 0.0))


def _route(logits, tm=512):
    T = logits.shape[0]
    tm = min(tm, T)
    row = pl.BlockSpec((tm, LANES), lambda i: (i, 0))
    return pl.pallas_call(
        _route_body,
        grid=(T // tm,),
        in_specs=[row],
        out_specs=[row, row, pl.BlockSpec((8, LANES), lambda i: (0, 0))],
        out_shape=[jax.ShapeDtypeStruct((T, LANES), I32), jax.ShapeDtypeStruct((T, LANES), F32),
                   jax.ShapeDtypeStruct((8, LANES), F32)],
        compiler_params=_cparams(("arbitrary",)),
        name="route",
    )(logits)


def _row_copy(src, s, dst, d, sem):
    return pltpu.make_async_copy(src.at[pl.ds(s, 1)], dst.at[pl.ds(d, 1)], sem)


def _dispatch_body(dest_ref, x_ref, xs_hbm, sem, *, td):
    def issue(t, carry):
        _row_copy(x_ref, t, xs_hbm, dest_ref[0, 0, t], sem).start()
        _row_copy(x_ref, t, xs_hbm, dest_ref[0, 1, t], sem).start()
        return carry

    def drain(t, carry):
        _row_copy(x_ref, 0, xs_hbm, 0, sem).wait()
        _row_copy(x_ref, 0, xs_hbm, 0, sem).wait()
        return carry

    lax.fori_loop(0, td, issue, 0)
    lax.fori_loop(0, td, drain, 0)


def _dispatch(x2, dest3, td):
    T, D = x2.shape
    return pl.pallas_call(
        functools.partial(_dispatch_body, td=td),
        grid=(T // td,),
        in_specs=[pl.BlockSpec((1, 2, td), lambda i: (i, 0, 0), memory_space=pltpu.SMEM),
                  pl.BlockSpec((td, D), lambda i: (i, 0))],
        out_specs=pl.BlockSpec(memory_space=pl.ANY),
        out_shape=jax.ShapeDtypeStruct((2 * T, D), x2.dtype),
        scratch_shapes=[pltpu.SemaphoreType.DMA(())],
        compiler_params=_cparams(("arbitrary",)),
        name="dispatch",
    )(dest3, x2)


def _expert_body(blk_ref, e_ref, lo_ref, hi_ref, first_ref, newe_ref, xs_ref, wg_ref, wu_ref, wd_ref,
                 o_ref, wg_scr, wu_scr, wd_scr):
    t = pl.program_id(0)
    lo = lo_ref[t]
    hi = hi_ref[t]

    @pl.when(newe_ref[t] == 1)
    def _():
        wg_scr[...] = wg_ref[0].astype(BF16)
        wu_scr[...] = wu_ref[0].astype(BF16)
        wd_scr[...] = wd_ref[0].astype(BF16)

    @pl.when(first_ref[t] == 1)
    def _():
        o_ref[...] = jnp.zeros(o_ref.shape, U32)

    @pl.when(hi > lo)
    def _():
        ha, hb = _unpack_halves(xs_ref[...])
        h = jnp.concatenate([ha.astype(BF16), hb.astype(BF16)], axis=1)
        g = jnp.dot(h, wg_scr[...], preferred_element_type=F32)
        u = jnp.dot(h, wu_scr[...], preferred_element_type=F32)
        hdn = (g * jax.nn.sigmoid(g) * u).astype(BF16)
        y = jnp.dot(hdn, wd_scr[...], preferred_element_type=F32)
        row = lax.broadcasted_iota(I32, (y.shape[0], 1), 0)
        keep = jnp.logical_and(row >= lo, row < hi)
        o_ref[...] = jnp.where(keep, _pack_halves(y[:, :_HALF], y[:, _HALF:]), o_ref[...])


def _expert_ffn(xs, wg, wu, wd, items, bm):
    A, DH = xs.shape
    D = 2 * DH
    blk, eid, lo, hi, first, newe = items
    ni = blk.shape[0]
    grid_spec = pltpu.PrefetchScalarGridSpec(
        num_scalar_prefetch=6,
        grid=(ni,),
        in_specs=[pl.BlockSpec((bm, DH), lambda t, b, e, lo, hi, f, n: (b[t], 0)),
                  pl.BlockSpec((1, D, D_EXPERT), lambda t, b, e, lo, hi, f, n: (e[t], 0, 0)),
                  pl.BlockSpec((1, D, D_EXPERT), lambda t, b, e, lo, hi, f, n: (e[t], 0, 0)),
                  pl.BlockSpec((1, D_EXPERT, D), lambda t, b, e, lo, hi, f, n: (e[t], 0, 0))],
        out_specs=pl.BlockSpec((bm, DH), lambda t, b, e, lo, hi, f, n: (b[t], 0)),
        scratch_shapes=[pltpu.VMEM((D, D_EXPERT), BF16), pltpu.VMEM((D, D_EXPERT), BF16),
                        pltpu.VMEM((D_EXPERT, D), BF16)],
    )
    return pl.pallas_call(
        _expert_body,
        grid_spec=grid_spec,
        out_shape=jax.ShapeDtypeStruct((A, DH), U32),
        compiler_params=_cparams(("arbitrary",)),
        name="expert_ffn",
    )(blk, eid, lo, hi, first, newe, xs, wg, wu, wd)


def _combine_body(dest_ref, x_ref, gate_ref, gf_ref, ys_hbm, o_ref, buf, sem, *, tc, final):
    def issue(t, carry):
        pltpu.make_async_copy(ys_hbm.at[pl.ds(dest_ref[0, 0, t], 1)], buf.at[0, pl.ds(t, 1)], sem).start()
        pltpu.make_async_copy(ys_hbm.at[pl.ds(dest_ref[0, 1, t], 1)], buf.at[1, pl.ds(t, 1)], sem).start()
        return carry

    def drain(t, carry):
        pltpu.make_async_copy(ys_hbm.at[pl.ds(0, 1)], buf.at[0, pl.ds(0, 1)], sem).wait()
        pltpu.make_async_copy(ys_hbm.at[pl.ds(0, 1)], buf.at[0, pl.ds(0, 1)], sem).wait()
        return carry

    lax.fori_loop(0, tc, issue, 0)
    lax.fori_loop(0, tc, drain, 0)
    g = gate_ref[...]
    y0a, y0b = _unpack_halves(buf[0])
    y1a, y1b = _unpack_halves(buf[1])
    g0 = g[:, 0:1]
    g1 = g[:, 1:2]
    out = x_ref[...] + jnp.concatenate([g0 * y0a + g1 * y1a, g0 * y0b + g1 * y1b], axis=1)
    if final:
        out = out * lax.rsqrt(jnp.mean(out * out, axis=-1, keepdims=True) + EPS) * gf_ref[...]
    o_ref[...] = out


def _combine(x2, dest3, gates, ys, g_final, tc, final):
    T, D = x2.shape
    return pl.pallas_call(
        functools.partial(_combine_body, tc=tc, final=final),
        grid=(T // tc,),
        in_specs=[pl.BlockSpec((1, 2, tc), lambda i: (i, 0, 0), memory_space=pltpu.SMEM),
                  pl.BlockSpec((tc, D), lambda i: (i, 0)),
                  pl.BlockSpec((tc, LANES), lambda i: (i, 0)),
                  pl.BlockSpec((1, D), lambda i: (0, 0)),
                  pl.BlockSpec(memory_space=pl.ANY)],
        out_specs=pl.BlockSpec((tc, D), lambda i: (i, 0)),
        out_shape=jax.ShapeDtypeStruct((T, D), F32),
        scratch_shapes=[pltpu.VMEM((2, tc, D // 2), U32), pltpu.SemaphoreType.DMA(())],
        compiler_params=_cparams(("arbitrary",)),
        name="combine",
    )(dest3, x2, gates, g_final, ys)


def _rope_tables(L):
    pos = jnp.arange(L, dtype=F32)

    def cs(dim):
        inv = ROPE_THETA ** (-jnp.arange(0, dim, 2, dtype=F32) / dim)
        ang = pos[:, None] * inv[None, :]
        return jnp.cos(ang), jnp.sin(ang)

    ch, sh = cs(HEAD_DIM)
    ci, si = cs(IDX_DIM)
    zi = jnp.zeros_like(si)
    scale = HEAD_DIM ** -0.5
    cos128 = jnp.concatenate([ch, ch], axis=1)
    sin128 = jnp.concatenate([-sh, sh], axis=1)
    tabs = [
        cos128 * scale, sin128 * scale,
        cos128, sin128,
        jnp.concatenate([ci, ci, ci, ci], axis=1),
        jnp.concatenate([-si, zi, -si, zi], axis=1),
        jnp.concatenate([zi, si, zi, si], axis=1),
    ]
    return jnp.stack(tabs, axis=0)


def _work_items(counts, A, bm):
    starts = jnp.cumsum(counts) - counts
    ends = starts + counts
    nblk = A // bm
    bstart = jnp.arange(nblk, dtype=I32) * bm
    estart = starts[1:].astype(I32)
    bpos = jnp.arange(nblk, dtype=I32) + jnp.sum(estart[None, :] < bstart[:, None], axis=1).astype(I32)
    epos = jnp.arange(N_EXPERTS - 1, dtype=I32) + jnp.minimum(estart // bm + 1, nblk)
    bounds = jnp.zeros((nblk + N_EXPERTS - 1,), I32).at[bpos].set(bstart).at[epos].set(estart)
    nxt = jnp.concatenate([bounds[1:], jnp.array([A], I32)])
    blk = jnp.minimum(bounds // bm, nblk - 1)
    eid = jnp.minimum(jnp.sum(ends[None, :] <= bounds[:, None], axis=1), N_EXPERTS - 1).astype(I32)
    lo = bounds - blk * bm
    hi = nxt - blk * bm
    first = jnp.concatenate([jnp.ones((1,), I32), (blk[1:] != blk[:-1]).astype(I32)])
    newe = jnp.concatenate([jnp.ones((1,), I32), (eid[1:] != eid[:-1]).astype(I32)])
    return blk.astype(I32), eid, lo.astype(I32), hi.astype(I32), first, newe


def _pick(n, prefs):
    for p in prefs:
        if n % p == 0:
            return p
    return n


def kernel(x, norm_mix, w_in, conv_w, conv_b, conv_ln_g, conv_ln_b, fox_fb, w_out, norm_ffn,
           w_router_group, b_router_group, w_router_expert, b_router_expert, w_gate, w_up, w_down,
           norm_final):
    B, L, D = x.shape
    T = B * L
    depth = w_in.shape[0]
    k_sel = min(TOPK_MAX, L // 4)
    tabs = _rope_tables(L)
    fox_tq = _pick(L, (256, 128))
    bm = _pick(2 * T, (512, 256))
    td = _pick(T, (256, 128))

    tk = _ATT_TK
    nk = L // tk
    fox_tk = min(_FOX_TK, fox_tq)
    segs_a = [("plain", 0, 2 * D_CONV, 0, 0, (), 1.0),
              ("plain", _O_QF, D_FOX, 1, 0, (), HEAD_DIM ** -0.5),
              ("plain", _O_KF, D_FOX, 1, D_FOX, (), 1.0)]
    outs_a = [(2 * D_CONV, BF16), (2 * D_FOX, BF16)]
    c_kd = D_DSA
    c_qi = c_kd + HEAD_DIM
    c_kk = c_qi + IDX_HEADS * IDX_DIM
    c_sm = c_kk + LANES
    segs_b = [("rope128", 0, D_DSA, 0, 0, (0, 1), 1.0),
              ("rope128", c_kd, HEAD_DIM, 1, 0, (2, 3), 1.0),
              ("rope64", c_qi, IDX_HEADS * IDX_DIM, 2, 0, (4, 5, 6), 1.0),
              ("rope64", c_kk, LANES, 3, 0, (4, 5, 6), 1.0),
              ("plain", c_sm, LANES, 4, 0, (), 1.0)]
    outs_b = [(D_DSA, BF16), (HEAD_DIM, BF16), (IDX_HEADS * IDX_DIM, BF16), (LANES, BF16), (LANES, F32)]

    xf = x.reshape(T, D)
    for l in range(depth):
        wl = w_in[l]
        w_a = wl[:, :_O_VF].astype(BF16)
        w_vf_t = wl[:, _O_VF:_O_FL].T.astype(BF16)
        w_ki = wl[:, _O_KI:_O_WI]
        pad = jnp.zeros((D, LANES - IDX_DIM - IDX_HEADS - FOX_HEADS), F32)
        w_b = jnp.concatenate([wl[:, _O_QD:_O_VD], wl[:, _O_QI:_O_KI], w_ki, w_ki,
                               w_ki, wl[:, _O_WI:_O_END], wl[:, _O_FL:_O_QD], pad], axis=1).astype(BF16)
        w_vd_t = wl[:, _O_VD:_O_QI].T.astype(BF16)
        w_wi_t = wl[:, _O_WI:_O_END].T.astype(BF16)
        fb_row = jnp.zeros((1, LANES), F32).at[0, _S_FL:_S_FL + FOX_HEADS].set(fox_fb[l])

        u, fqk, fvt = _proj(xf, norm_mix[l], w_a, tabs, segs_a, outs_a, L,
                            t_groups=[(w_vf_t, fox_tk, BF16)])
        qd, kd, qi, kdup, small, dvt, wit = _proj(xf, norm_mix[l], w_b, tabs, segs_b, outs_b, L,
                                                 t_groups=[(w_vd_t, tk, BF16), (w_wi_t, tk, F32)])

        c_tm = _fox_cumsum(small.reshape(B, L, LANES), fb_row)
        ya = _conv_module(u.reshape(B, L, -1), conv_w[l], conv_b[l], conv_ln_g[l], conv_ln_b[l])
        yb = _fox_attention(fqk.reshape(B, L, -1), fvt.reshape(B, L // fox_tk, D_FOX, fox_tk), c_tm,
                            tq=fox_tq)
        yc = _dsa_attention(qd.reshape(B, L, -1), kd.reshape(B, L, -1), qi.reshape(B, L, -1),
                            kdup.reshape(B, L, -1), dvt.reshape(B, nk, HEAD_DIM, tk), wit, k_sel)

        wr = jnp.concatenate([w_router_group[l], w_router_expert[l],
                              jnp.zeros((D, LANES - N_GROUPS - N_EXPERTS), F32)], axis=1)
        wr_hi = wr.astype(BF16)
        wr_lo = (wr - wr_hi.astype(F32)).astype(BF16)
        br = jnp.concatenate([b_router_group[l], b_router_expert[l],
                              jnp.zeros((LANES - N_GROUPS - N_EXPERTS,), F32)]).reshape(1, LANES)
        g_ffn = norm_ffn[l].reshape(1, D)
        x2, logits, hpk = _outproj(xf, ya.reshape(T, -1), yb.reshape(T, -1), yc.reshape(T, -1),
                                   w_out[l].astype(BF16), g_ffn, wr_hi, wr_lo, br)

        info, gates, cnt = _route(logits)
        counts = cnt[0, :N_EXPERTS].astype(I32)
        starts = jnp.cumsum(counts) - counts
        dest = starts[info[:, 0:2]] + info[:, 2:4]
        dest3 = dest.reshape(T // td, td, 2).transpose(0, 2, 1)

        xs = _dispatch(hpk, dest3, td)
        items = _work_items(counts, 2 * T, bm)
        ys = _expert_ffn(xs, w_gate[l], w_up[l], w_down[l], items, bm)
        xf = _combine(x2, dest3, gates, ys, norm_final.reshape(1, D), td, final=(l == depth - 1))
    return xf.reshape(B, L, D)
```

```python
import functools

import jax
import jax.numpy as jnp
import numpy as np
from jax import lax
from jax.experimental import pallas as pl
from jax.experimental.pallas import tpu as pltpu

F32 = jnp.float32
BF16 = jnp.bfloat16
I32 = jnp.int32

D_MODEL = 2048
HEAD_DIM = 128
D_CONV = 512
CONV_WIDTH = 31
FOX_HEADS = 6
DSA_HEADS = 6
D_FOX = FOX_HEADS * HEAD_DIM
D_DSA = DSA_HEADS * HEAD_DIM
IDX_HEADS = 16
IDX_DIM = 64
TOPK_MAX = 256
ROPE_THETA = 10000.0
CHUNK = 64
N_GROUPS = 4
EXPERTS_PER_GROUP = 8
N_EXPERTS = N_GROUPS * EXPERTS_PER_GROUP
D_EXPERT = 512
EPS = 1e-6

LANES = 128
NEG = -1e30
INT_MIN = -(2 ** 31)
VMEM_LIMIT = 56 * 1024 * 1024

_O_U = 0
_O_QF = _O_U + 2 * D_CONV
_O_KF = _O_QF + D_FOX
_O_VF = _O_KF + D_FOX
_O_FL = _O_VF + D_FOX
_O_QD = _O_FL + FOX_HEADS
_O_KD = _O_QD + D_DSA
_O_VD = _O_KD + HEAD_DIM
_O_QI = _O_VD + HEAD_DIM
_O_KI = _O_QI + IDX_HEADS * IDX_DIM
_O_WI = _O_KI + IDX_DIM
_O_END = _O_WI + IDX_HEADS

_S_KI = 0
_S_WI = IDX_DIM
_S_FL = IDX_DIM + IDX_HEADS


def _cparams(sem):
    return pltpu.CompilerParams(dimension_semantics=sem, vmem_limit_bytes=VMEM_LIMIT)


def _proj_body(x_ref, g_ref, w_ref, tab_ref, *rest, segs, n_t, n_out):
    wts, outs, outs_t = rest[:n_t], rest[n_t:n_t + n_out], rest[n_t + n_out:]
    x = x_ref[...]
    inv = lax.rsqrt(jnp.mean(x * x, axis=-1, keepdims=True) + EPS)
    h = (x * inv * g_ref[...]).astype(BF16)
    for (kind, w0, width, oi, o0, tabs, scale) in segs:
        z = jnp.dot(h, w_ref[:, w0:w0 + width], preferred_element_type=F32)
        for c in range(width // LANES):
            zc = z[:, c * LANES:(c + 1) * LANES]
            if kind == "plain":
                if scale != 1.0:
                    zc = zc * scale
            elif kind == "rope128":
                zc = zc * tab_ref[tabs[0]] + pltpu.roll(zc, 64, 1) * tab_ref[tabs[1]]
            else:
                zc = (zc * tab_ref[tabs[0]] + pltpu.roll(zc, 96, 1) * tab_ref[tabs[1]]
                      + pltpu.roll(zc, 32, 1) * tab_ref[tabs[2]])
            outs[oi][:, o0 + c * LANES:o0 + (c + 1) * LANES] = zc.astype(outs[oi].dtype)
    for wt_ref, o_ref in zip(wts, outs_t):
        zt = lax.dot_general(wt_ref[...], h, (((1,), (1,)), ((), ())), preferred_element_type=F32)
        tw = o_ref.shape[2]
        for c in range(o_ref.shape[0]):
            o_ref[c] = zt[:, c * tw:(c + 1) * tw].astype(o_ref.dtype)


def _proj(x2d, g, w, tabs, segs, out_defs, seq_len, t_groups=(), tm=512):
    T, D = x2d.shape
    tm = min(tm, seq_len)
    nt = T // tm
    nl = seq_len // tm
    ntab = tabs.shape[0]
    in_specs = [
        pl.BlockSpec((tm, D), lambda i: (i, 0)),
        pl.BlockSpec((1, D), lambda i: (0, 0)),
        pl.BlockSpec(w.shape, lambda i: (0, 0)),
        pl.BlockSpec((ntab, tm, LANES), lambda i: (0, i % nl, 0)),
    ]
    args = [x2d, g.reshape(1, D), w, tabs]
    out_shape = [jax.ShapeDtypeStruct((T, n), dt) for (n, dt) in out_defs]
    out_specs = [pl.BlockSpec((tm, n), lambda i: (i, 0)) for (n, dt) in out_defs]
    for (wt, tw, dt) in t_groups:
        in_specs.append(pl.BlockSpec(wt.shape, lambda i: (0, 0)))
        args.append(wt)
    for (wt, tw, dt) in t_groups:
        out_shape.append(jax.ShapeDtypeStruct((T // tw, wt.shape[0], tw), dt))
        out_specs.append(pl.BlockSpec((tm // tw, wt.shape[0], tw), lambda i: (i, 0, 0)))
    return pl.pallas_call(
        functools.partial(_proj_body, segs=tuple(segs), n_t=len(t_groups), n_out=len(out_defs)),
        grid=(nt,),
        in_specs=in_specs,
        out_specs=out_specs,
        out_shape=out_shape,
        compiler_params=_cparams(("parallel",)),
        name="proj",
    )(*args)


def _cumsum_body(fl_ref, fb_ref, o_ref):
    z = fl_ref[0] + fb_ref[...]
    x = jnp.minimum(z, 0.0) - jnp.log(1.0 + jnp.exp(-jnp.abs(z)))
    L = x.shape[0]
    row = lax.broadcasted_iota(I32, x.shape, 0)
    s = 1
    while s < L:
        x = x + jnp.where(row >= s, pltpu.roll(x, s, 0), 0.0)
        s *= 2
    o_ref[0] = x


def _fox_cumsum(small, fb_row):
    B, L, _ = small.shape
    return pl.pallas_call(
        _cumsum_body,
        grid=(B,),
        in_specs=[pl.BlockSpec((1, L, LANES), lambda b: (b, 0, 0)),
                  pl.BlockSpec((1, LANES), lambda b: (0, 0))],
        out_specs=pl.BlockSpec((1, L, LANES), lambda b: (b, 0, 0)),
        out_shape=jax.ShapeDtypeStruct((B, L, LANES), F32),
        compiler_params=_cparams(("parallel",)),
        name="fox_cumsum",
    )(small, fb_row)


_CONV_HALO = 32


def _conv_body(u_ref, w_ref, cb_ref, g_ref, b_ref, o_ref, a_scr, *, rc):
    L = u_ref.shape[1]
    a_scr[0:_CONV_HALO, :] = jnp.zeros((_CONV_HALO, D_CONV), F32)
    u1 = u_ref[0, :, 0:D_CONV].astype(F32)
    u2 = u_ref[0, :, D_CONV:2 * D_CONV].astype(F32)
    a_scr[_CONV_HALO:_CONV_HALO + L, :] = u1 * jax.nn.sigmoid(u2)
    win = rc + _CONV_HALO

    def chunk(r, carry):
        base = pl.multiple_of(r * rc, 8)
        wnd = a_scr[pl.ds(base, win), :]
        acc = jnp.zeros((rc, D_CONV), F32) + cb_ref[...]
        for j in range(CONV_WIDTH):
            off = _CONV_HALO - (CONV_WIDTH - 1) + j
            sh = pltpu.roll(wnd, win - off, 0)[0:rc]
            acc = acc + sh * w_ref[j:j + 1, :]
        mu = jnp.mean(acc, axis=-1, keepdims=True)
        d = acc - mu
        var = jnp.mean(d * d, axis=-1, keepdims=True)
        y = d * lax.rsqrt(var + EPS) * g_ref[...] + b_ref[...]
        o_ref[0, pl.ds(base, rc), :] = (y * jax.nn.sigmoid(y)).astype(o_ref.dtype)
        return carry

    lax.fori_loop(0, L // rc, chunk, 0)


def _conv_module(u, conv_w, conv_b, ln_g, ln_b, rc=256):
    B, L, _ = u.shape
    rc = min(rc, L)
    return pl.pallas_call(
        functools.partial(_conv_body, rc=rc),
        grid=(B,),
        in_specs=[pl.BlockSpec((1, L, 2 * D_CONV), lambda b: (b, 0, 0)),
                  pl.BlockSpec((CONV_WIDTH, D_CONV), lambda b: (0, 0)),
                  pl.BlockSpec((1, D_CONV), lambda b: (0, 0)),
                  pl.BlockSpec((1, D_CONV), lambda b: (0, 0)),
                  pl.BlockSpec((1, D_CONV), lambda b: (0, 0))],
        out_specs=pl.BlockSpec((1, L, D_CONV), lambda b: (b, 0, 0)),
        out_shape=jax.ShapeDtypeStruct((B, L, D_CONV), BF16),
        scratch_shapes=[pltpu.VMEM((L + _CONV_HALO, D_CONV), F32)],
        compiler_params=_cparams(("parallel",)),
        name="conv_module",
    )(u, conv_w, conv_b.reshape(1, -1), ln_g.reshape(1, -1), ln_b.reshape(1, -1))


_ATT_TK = 128
_FOX_TK = 128


def _fox_body(q_ref, k_ref, vt_ref, c_ref, o_ref, cb_scr, acc_scr, *, tq, tk):
    qi = pl.program_id(1)
    L = k_ref.shape[1]
    H = FOX_HEADS
    nsub = tq // tk

    @pl.when(qi == 0)
    def _():
        def fill(r, carry):
            off = pl.multiple_of(r * tk, tk)
            c = c_ref[0, pl.ds(off, tk), :]
            for h in range(H):
                cb_scr[h, pl.ds(off, tk), :] = jnp.broadcast_to(
                    c[:, _S_FL + h:_S_FL + h + 1], (tk, LANES))
            return carry

        lax.fori_loop(0, L // tk, fill, 0)

    q = [q_ref[0, :, h * HEAD_DIM:(h + 1) * HEAD_DIM] for h in range(H)]
    acc_scr[...] = jnp.zeros(acc_scr.shape, F32)

    def tile(j, carry, diag):
        ms, ls = carry
        off = pl.multiple_of(j * tk, tk)
        new_m, new_l, alphas, ps = [], [], [], []
        ss = [lax.dot_general(k_ref[0, pl.ds(off, tk), h * HEAD_DIM:(h + 1) * HEAD_DIM], q[h],
                              (((1,), (1,)), ((), ())), preferred_element_type=F32) for h in range(H)]
        for h in range(H):
            cb = cb_scr[h, pl.ds(off, tk), :]
            s = ss[h] - jnp.concatenate([cb] * (tq // LANES), axis=1)
            if diag:
                kk = off + lax.broadcasted_iota(I32, s.shape, 0)
                qq = qi * tq + lax.broadcasted_iota(I32, s.shape, 1)
                s = jnp.where(kk <= qq, s, NEG)
            m_new = jnp.maximum(ms[h], jnp.max(s, axis=0, keepdims=True))
            alpha = jnp.exp(ms[h] - m_new)
            p = jnp.exp(s - m_new)
            new_l.append(ls[h] * alpha + jnp.sum(p, axis=0, keepdims=True))
            new_m.append(m_new)
            alphas.append(alpha)
            ps.append(p.astype(BF16))
        pvs = [jnp.dot(vt_ref[0, j, h * HEAD_DIM:(h + 1) * HEAD_DIM, :], ps[h],
                       preferred_element_type=F32) for h in range(H)]
        for h in range(H):
            acc_scr[h] = acc_scr[h] * alphas[h] + pvs[h]
        return tuple(new_m), tuple(new_l)

    init = (tuple(jnp.full((1, tq), NEG, F32) for _ in range(H)),
            tuple(jnp.zeros((1, tq), F32) for _ in range(H)))
    carry = lax.fori_loop(0, qi * nsub, functools.partial(tile, diag=False), init)
    for d in range(nsub):
        carry = tile(qi * nsub + d, carry, True)
    ms, ls = carry
    for h in range(H):
        out = acc_scr[h] * (1.0 / ls[h])
        o_ref[0, :, h * HEAD_DIM:(h + 1) * HEAD_DIM] = out.T.astype(o_ref.dtype)


def _fox_attention(qk, vt, c_tm, tq=256):
    B, L, _ = qk.shape
    tq = min(tq, L)
    nq = L // tq
    nk, tk = vt.shape[1], vt.shape[3]
    return pl.pallas_call(
        functools.partial(_fox_body, tq=tq, tk=tk),
        grid=(B, nq),
        in_specs=[pl.BlockSpec((1, tq, D_FOX), lambda b, i: (b, i, 0)),
                  pl.BlockSpec((1, L, D_FOX), lambda b, i: (b, 0, 1)),
                  pl.BlockSpec((1, nk, D_FOX, tk), lambda b, i: (b, 0, 0, 0)),
                  pl.BlockSpec((1, L, LANES), lambda b, i: (b, 0, 0))],
        out_specs=pl.BlockSpec((1, tq, D_FOX), lambda b, i: (b, i, 0)),
        out_shape=jax.ShapeDtypeStruct((B, L, D_FOX), BF16),
        scratch_shapes=[pltpu.VMEM((FOX_HEADS, L, LANES), F32),
                        pltpu.VMEM((FOX_HEADS, HEAD_DIM, tq), F32)],
        compiler_params=_cparams(("arbitrary", "arbitrary")),
        name="fox_attention",
    )(qk, qk, vt, c_tm)


def _dsa_body(qd_ref, qi_ref, wt_ref, k_ref, kdup_ref, vt_ref, o_ref,
              key_scr, rhs_scr, acc_scr, thr_scr, *, tq, k_sel, nvs):
    qb = pl.program_id(1)
    H = DSA_HEADS
    half = IDX_DIM

    lane = lax.broadcasted_iota(I32, (tq, LANES), 1)
    for p in range(IDX_HEADS // 2):
        qp = qi_ref[0, :, p * LANES:(p + 1) * LANES].astype(F32)
        rhs_scr[(2 * p) * tq:(2 * p + 1) * tq, :] = jnp.where(lane < half, qp, 0.0).astype(BF16)
        rhs_scr[(2 * p + 1) * tq:(2 * p + 2) * tq, :] = jnp.where(lane >= half, qp, 0.0).astype(BF16)
    wq = wt_ref[0] * ((IDX_DIM ** -0.5) * (IDX_HEADS ** -0.5))

    key_scr[...] = jnp.full(key_scr.shape, INT_MIN, I32)
    rowk = lax.broadcasted_iota(I32, (tq, tq), 0)
    colq = lax.broadcasted_iota(I32, (tq, tq), 1)
    admissible_diag = (rowk // CHUNK) <= (colq // CHUNK)

    npair = (qb + 2) // 2

    def score_pair(jj, carry):
        for g in range(2):
            j = 2 * jj + g
            off = pl.multiple_of(j * tq, tq)
            kd = kdup_ref[0, pl.ds(off, tq), :]
            s = jnp.zeros((tq, tq), F32)
            for p in range(IDX_HEADS // 2):
                r = lax.dot_general(kd, rhs_scr[(2 * p) * tq:(2 * p + 2) * tq, :],
                                    (((1,), (1,)), ((), ())), preferred_element_type=F32)
                s = s + jnp.maximum(r[:, :tq], 0.0) * wq[2 * p:2 * p + 1, :]
                s = s + jnp.maximum(r[:, tq:], 0.0) * wq[2 * p + 1:2 * p + 2, :]
            bits = pltpu.bitcast(s, I32)
            key = jnp.where(bits < 0, bits ^ jnp.int32(0x7FFFFFFF), bits)
            adm = jnp.logical_or(j < qb, jnp.logical_and(j == qb, admissible_diag))
            key_scr[j] = jnp.where(adm, key, INT_MIN)
        return carry

    lax.fori_loop(0, npair, score_pair, 0)

    def select(nv):
        def bit_step(i, t):
            cand = t + (jnp.int32(1) << (31 - i))
            cnt = jnp.sum(jnp.where(key_scr[0:nv] >= cand[None], 1.0, 0.0), axis=0)
            cnt = jnp.sum(cnt, axis=0, keepdims=True)
            return jnp.where(cnt >= float(k_sel), cand, t)

        t = lax.fori_loop(0, 32, bit_step, jnp.full((1, tq), INT_MIN, I32))
        t = jnp.maximum(t, INT_MIN + 1)
        thr_scr[...] = jnp.broadcast_to(t, thr_scr.shape)

    for idx, nv in enumerate(nvs):
        lo = nvs[idx - 1] if idx else 0

        @pl.when(jnp.logical_and(qb + 1 > lo, qb + 1 <= nv))
        def _(nv=nv):
            select(nv)

    thr = thr_scr[0:1, :]
    qs = jnp.concatenate([qd_ref[0, :, h * HEAD_DIM:(h + 1) * HEAD_DIM] for h in range(H)], axis=0)
    acc_scr[...] = jnp.zeros(acc_scr.shape, F32)

    def attn_pair(jj, carry):
        m, l = carry
        off = pl.multiple_of(jj * (2 * tq), 2 * tq)
        k = k_ref[0, pl.ds(off, 2 * tq), :]
        s = lax.dot_general(k, qs, (((1,), (1,)), ((), ())), preferred_element_type=F32)
        sel = key_scr[pl.ds(2 * jj, 2)].reshape(2 * tq, tq) >= thr
        s = jnp.concatenate([jnp.where(sel, s[:, h * tq:(h + 1) * tq], NEG) for h in range(H)], axis=1)
        m_new = jnp.maximum(m, jnp.max(s, axis=0, keepdims=True))
        alpha = jnp.exp(m - m_new)
        p = jnp.exp(s - m_new)
        l = l * alpha + jnp.sum(p, axis=0, keepdims=True)
        vt = jnp.concatenate([vt_ref[0, 2 * jj], vt_ref[0, 2 * jj + 1]], axis=1)
        acc_scr[...] = acc_scr[...] * alpha + jnp.dot(vt, p.astype(BF16), preferred_element_type=F32)
        return m_new, l

    init = (jnp.full((1, H * tq), NEG, F32), jnp.zeros((1, H * tq), F32))
    m, l = lax.fori_loop(0, npair, attn_pair, init)
    out = acc_scr[...] * (1.0 / l)
    for h in range(H):
        o_ref[0, :, h * HEAD_DIM:(h + 1) * HEAD_DIM] = out[:, h * tq:(h + 1) * tq].T.astype(o_ref.dtype)


def _dsa_attention(qd, kd, qi, kdup, vt, wit, k_sel):
    B, L, _ = qd.shape
    tq = _ATT_TK
    nq = L // tq
    nvs = tuple(sorted({-(-nq * f // 4) for f in (1, 2, 3, 4)}))
    return pl.pallas_call(
        functools.partial(_dsa_body, tq=tq, k_sel=k_sel, nvs=nvs),
        grid=(B, nq),
        in_specs=[pl.BlockSpec((1, tq, D_DSA), lambda b, i: (b, i, 0)),
                  pl.BlockSpec((1, tq, IDX_HEADS * IDX_DIM), lambda b, i: (b, i, 0)),
                  pl.BlockSpec((1, IDX_HEADS, tq), lambda b, i: (b * nq + i, 0, 0)),
                  pl.BlockSpec((1, L, HEAD_DIM), lambda b, i: (b, 0, 0)),
                  pl.BlockSpec((1, L, LANES), lambda b, i: (b, 0, 0)),
                  pl.BlockSpec((1, nq, HEAD_DIM, tq), lambda b, i: (b, 0, 0, 0))],
        out_specs=pl.BlockSpec((1, tq, D_DSA), lambda b, i: (b, i, 0)),
        out_shape=jax.ShapeDtypeStruct((B, L, D_DSA), BF16),
        scratch_shapes=[pltpu.VMEM((nq, tq, tq), I32),
                        pltpu.VMEM((IDX_HEADS * tq, LANES), BF16),
                        pltpu.VMEM((HEAD_DIM, DSA_HEADS * tq), F32),
                        pltpu.VMEM((8, tq), I32)],
        compiler_params=_cparams(("arbitrary", "arbitrary")),
        name="dsa_attention",
    )(qd, qi, wit, kd, kdup, vt)


U32 = jnp.uint32
_HALF = D_MODEL // 2


def _pack_halves(a, b):
    ah = pltpu.bitcast(a.astype(BF16).astype(F32), U32)
    bh = pltpu.bitcast(b.astype(BF16).astype(F32), U32)
    return ah | (bh >> 16)


def _unpack_halves(u):
    a = pltpu.bitcast(u & jnp.uint32(0xFFFF0000), F32)
    b = pltpu.bitcast(u << 16, F32)
    return a, b


def _outproj_body(x_ref, ya_ref, yb_ref, yc_ref, w_ref, g_ref, wrh_ref, wrl_ref, br_ref,
                  x2_ref, lg_ref, hp_ref):
    acc = x_ref[...]
    acc = acc + jnp.dot(ya_ref[...], w_ref[0:D_CONV, :], preferred_element_type=F32)
    acc = acc + jnp.dot(yb_ref[...], w_ref[D_CONV:D_CONV + D_FOX, :], preferred_element_type=F32)
    acc = acc + jnp.dot(yc_ref[...], w_ref[D_CONV + D_FOX:, :], preferred_element_type=F32)
    x2_ref[...] = acc
    h = acc * lax.rsqrt(jnp.mean(acc * acc, axis=-1, keepdims=True) + EPS) * g_ref[...]
    hh = h.astype(BF16)
    hl = (h - hh.astype(F32)).astype(BF16)
    lg = jnp.dot(hh, wrh_ref[...], preferred_element_type=F32)
    lg = lg + jnp.dot(hh, wrl_ref[...], preferred_element_type=F32)
    lg = lg + jnp.dot(hl, wrh_ref[...], preferred_element_type=F32)
    lg_ref[...] = lg + br_ref[...]
    hp_ref[...] = _pack_halves(h[:, :_HALF], h[:, _HALF:])


def _outproj(x2d, ya, yb, yc, w_out, g_ffn, wr_hi, wr_lo, br, tm=512):
    T, D = x2d.shape
    tm = min(tm, T)
    row = lambda n: pl.BlockSpec((tm, n), lambda i: (i, 0))
    const = lambda a: pl.BlockSpec(a.shape, lambda i: (0, 0))
    return pl.pallas_call(
        _outproj_body,
        grid=(T // tm,),
        in_specs=[row(D), row(D_CONV), row(D_FOX), row(D_DSA), const(w_out), const(g_ffn),
                  const(wr_hi), const(wr_lo), const(br)],
        out_specs=[row(D), row(LANES), row(_HALF)],
        out_shape=[jax.ShapeDtypeStruct((T, D), F32), jax.ShapeDtypeStruct((T, LANES), F32),
                   jax.ShapeDtypeStruct((T, _HALF), U32)],
        compiler_params=_cparams(("parallel",)),
        name="outproj",
    )(x2d, ya, yb, yc, w_out, g_ffn, wr_hi, wr_lo, br)


def _route_body(lg_ref, info_ref, gate_ref, cnt_ref):
    i = pl.program_id(0)
    lg = lg_ref[...]
    tm = lg.shape[0]
    lane = lax.broadcasted_iota(I32, lg.shape, 1)
    ninf = -jnp.inf

    lane_f = lane.astype(F32)

    def first_lane(mask):
        return jnp.min(jnp.where(mask, lane_f, float(LANES)), axis=-1, keepdims=True).astype(I32)

    gl = jnp.where(lane < N_GROUPS, lg, ninf)
    gexp = jnp.exp(gl - jnp.max(gl, axis=-1, keepdims=True))
    gprob = gexp / jnp.sum(gexp, axis=-1, keepdims=True)
    p_g = jnp.max(gprob, axis=-1, keepdims=True)
    g_idx = first_lane(gprob == p_g)

    e_lo = N_GROUPS + EXPERTS_PER_GROUP * g_idx
    emask = jnp.logical_and(lane >= e_lo, lane < e_lo + EXPERTS_PER_GROUP)
    el = jnp.where(emask, lg, ninf)
    eexp = jnp.exp(el - jnp.max(el, axis=-1, keepdims=True))
    eprob = jnp.where(emask, eexp / jnp.sum(eexp, axis=-1, keepdims=True), -1.0)
    p1 = jnp.max(eprob, axis=-1, keepdims=True)
    l1 = first_lane(eprob == p1)
    eprob2 = jnp.where(lane == l1, -1.0, eprob)
    p2 = jnp.max(eprob2, axis=-1, keepdims=True)
    l2 = first_lane(eprob2 == p2)
    den = p1 + p2
    gate1 = p_g * p1 / den
    gate2 = p_g * p2 / den
    eid1 = l1 - N_GROUPS
    eid2 = l2 - N_GROUPS

    @pl.when(i == 0)
    def _():
        cnt_ref[...] = jnp.zeros(cnt_ref.shape, F32)

    hot1 = lane == eid1
    hot2 = lane == eid2
    onehot = jnp.where(jnp.logical_or(hot1, hot2), 1.0, 0.0)
    r = lax.broadcasted_iota(I32, (tm, tm), 0)
    c = lax.broadcasted_iota(I32, (tm, tm), 1)
    tri = jnp.where(c < r, 1.0, 0.0).astype(BF16)
    before = jnp.dot(tri, onehot.astype(BF16), preferred_element_type=F32) + cnt_ref[0:1, :]
    rank1 = jnp.sum(jnp.where(hot1, before, 0.0), axis=-1, keepdims=True).astype(I32)
    rank2 = jnp.sum(jnp.where(hot2, before, 0.0), axis=-1, keepdims=True).astype(I32)
    cnt_ref[...] = cnt_ref[...] + jnp.sum(onehot, axis=0, keepdims=True)

    info_ref[...] = jnp.where(lane == 0, eid1, jnp.where(lane == 1, eid2,
                              jnp.where(lane == 2, rank1, jnp.where(lane == 3, rank2, 0))))
    gate_ref[...] = jnp.where(lane == 0, gate1, jnp.where(lane == 1, gate2, 0.0))


def _route(logits, tm=512):
    T = logits.shape[0]
    tm = min(tm, T)
    row = pl.BlockSpec((tm, LANES), lambda i: (i, 0))
    return pl.pallas_call(
        _route_body,
        grid=(T // tm,),
        in_specs=[row],
        out_specs=[row, row, pl.BlockSpec((8, LANES), lambda i: (0, 0))],
        out_shape=[jax.ShapeDtypeStruct((T, LANES), I32), jax.ShapeDtypeStruct((T, LANES), F32),
                   jax.ShapeDtypeStruct((8, LANES), F32)],
        compiler_params=_cparams(("arbitrary",)),
        name="route",
    )(logits)


def _row_copy(src, s, dst, d, sem):
    return pltpu.make_async_copy(src.at[pl.ds(s, 1)], dst.at[pl.ds(d, 1)], sem)


def _dispatch_body(dest_ref, x_ref, xs_hbm, sem, *, td):
    def issue(t, carry):
        _row_copy(x_ref, t, xs_hbm, dest_ref[0, 0, t], sem).start()
        _row_copy(x_ref, t, xs_hbm, dest_ref[0, 1, t], sem).start()
        return carry

    lax.fori_loop(0, td, issue, 0)
    pltpu.make_async_copy(xs_hbm.at[pl.ds(0, 2 * td)], xs_hbm.at[pl.ds(0, 2 * td)], sem).wait()


def _dispatch(x2, dest3, td):
    T, D = x2.shape
    return pl.pallas_call(
        functools.partial(_dispatch_body, td=td),
        grid=(T // td,),
        in_specs=[pl.BlockSpec((1, 2, td), lambda i: (i, 0, 0), memory_space=pltpu.SMEM),
                  pl.BlockSpec((td, D), lambda i: (i, 0))],
        out_specs=pl.BlockSpec(memory_space=pl.ANY),
        out_shape=jax.ShapeDtypeStruct((2 * T, D), x2.dtype),
        scratch_shapes=[pltpu.SemaphoreType.DMA(())],
        compiler_params=_cparams(("arbitrary",)),
        name="dispatch",
    )(dest3, x2)


def _expert_body(blk_ref, e_ref, lo_ref, hi_ref, first_ref, newe_ref, xs_ref, wg_ref, wu_ref, wd_ref,
                 o_ref, wg_scr, wu_scr, wd_scr):
    t = pl.program_id(0)
    lo = lo_ref[t]
    hi = hi_ref[t]

    @pl.when(newe_ref[t] == 1)
    def _():
        wg_scr[...] = wg_ref[0].astype(BF16)
        wu_scr[...] = wu_ref[0].astype(BF16)
        wd_scr[...] = wd_ref[0].astype(BF16)

    @pl.when(first_ref[t] == 1)
    def _():
        o_ref[...] = jnp.zeros(o_ref.shape, U32)

    @pl.when(hi > lo)
    def _():
        ha, hb = _unpack_halves(xs_ref[...])
        h = jnp.concatenate([ha.astype(BF16), hb.astype(BF16)], axis=1)
        g = jnp.dot(h, wg_scr[...], preferred_element_type=F32)
        u = jnp.dot(h, wu_scr[...], preferred_element_type=F32)
        hdn = (g * jax.nn.sigmoid(g) * u).astype(BF16)
        y = jnp.dot(hdn, wd_scr[...], preferred_element_type=F32)
        row = lax.broadcasted_iota(I32, (y.shape[0], 1), 0)
        keep = jnp.logical_and(row >= lo, row < hi)
        o_ref[...] = jnp.where(keep, _pack_halves(y[:, :_HALF], y[:, _HALF:]), o_ref[...])


def _expert_ffn(xs, wg, wu, wd, layer, items, bm):
    A, DH = xs.shape
    D = 2 * DH
    blk, eid, lo, hi, first, newe = items
    ni = blk.shape[0]
    grid_spec = pltpu.PrefetchScalarGridSpec(
        num_scalar_prefetch=6,
        grid=(ni,),
        in_specs=[pl.BlockSpec((bm, DH), lambda t, b, e, lo, hi, f, n: (b[t], 0)),
                  pl.BlockSpec((None, 1, D, D_EXPERT), lambda t, b, e, lo, hi, f, n: (layer, e[t], 0, 0)),
                  pl.BlockSpec((None, 1, D, D_EXPERT), lambda t, b, e, lo, hi, f, n: (layer, e[t], 0, 0)),
                  pl.BlockSpec((None, 1, D_EXPERT, D), lambda t, b, e, lo, hi, f, n: (layer, e[t], 0, 0))],
        out_specs=pl.BlockSpec((bm, DH), lambda t, b, e, lo, hi, f, n: (b[t], 0)),
        scratch_shapes=[pltpu.VMEM((D, D_EXPERT), BF16), pltpu.VMEM((D, D_EXPERT), BF16),
                        pltpu.VMEM((D_EXPERT, D), BF16)],
    )
    return pl.pallas_call(
        _expert_body,
        grid_spec=grid_spec,
        out_shape=jax.ShapeDtypeStruct((A, DH), U32),
        compiler_params=_cparams(("arbitrary",)),
        name="expert_ffn",
    )(blk, eid, lo, hi, first, newe, xs, wg, wu, wd)


def _combine_body(dest_ref, x_ref, gate_ref, gf_ref, ys_hbm, o_ref, buf, sem, *, tc, final):
    def issue(t, carry):
        pltpu.make_async_copy(ys_hbm.at[pl.ds(dest_ref[0, 0, t], 1)], buf.at[0, pl.ds(t, 1)], sem).start()
        pltpu.make_async_copy(ys_hbm.at[pl.ds(dest_ref[0, 1, t], 1)], buf.at[1, pl.ds(t, 1)], sem).start()
        return carry

    lax.fori_loop(0, tc, issue, 0)
    pltpu.make_async_copy(buf, buf, sem).wait()
    g = gate_ref[...]
    y0a, y0b = _unpack_halves(buf[0])
    y1a, y1b = _unpack_halves(buf[1])
    g0 = g[:, 0:1]
    g1 = g[:, 1:2]
    out = x_ref[...] + jnp.concatenate([g0 * y0a + g1 * y1a, g0 * y0b + g1 * y1b], axis=1)
    if final:
        out = out * lax.rsqrt(jnp.mean(out * out, axis=-1, keepdims=True) + EPS) * gf_ref[...]
    o_ref[...] = out


def _combine(x2, dest3, gates, ys, g_final, tc, final):
    T, D = x2.shape
    return pl.pallas_call(
        functools.partial(_combine_body, tc=tc, final=final),
        grid=(T // tc,),
        in_specs=[pl.BlockSpec((1, 2, tc), lambda i: (i, 0, 0), memory_space=pltpu.SMEM),
                  pl.BlockSpec((tc, D), lambda i: (i, 0)),
                  pl.BlockSpec((tc, LANES), lambda i: (i, 0)),
                  pl.BlockSpec((1, D), lambda i: (0, 0)),
                  pl.BlockSpec(memory_space=pl.ANY)],
        out_specs=pl.BlockSpec((tc, D), lambda i: (i, 0)),
        out_shape=jax.ShapeDtypeStruct((T, D), F32),
        scratch_shapes=[pltpu.VMEM((2, tc, D // 2), U32), pltpu.SemaphoreType.DMA(())],
        compiler_params=_cparams(("arbitrary",)),
        name="combine",
    )(dest3, x2, gates, g_final, ys)


def _rope_tables(L):
    pos = jnp.arange(L, dtype=F32)

    def cs(dim):
        inv = ROPE_THETA ** (-jnp.arange(0, dim, 2, dtype=F32) / dim)
        ang = pos[:, None] * inv[None, :]
        return jnp.cos(ang), jnp.sin(ang)

    ch, sh = cs(HEAD_DIM)
    ci, si = cs(IDX_DIM)
    zi = jnp.zeros_like(si)
    scale = HEAD_DIM ** -0.5
    cos128 = jnp.concatenate([ch, ch], axis=1)
    sin128 = jnp.concatenate([-sh, sh], axis=1)
    tabs = [
        cos128 * scale, sin128 * scale,
        cos128, sin128,
        jnp.concatenate([ci, ci, ci, ci], axis=1),
        jnp.concatenate([-si, zi, -si, zi], axis=1),
        jnp.concatenate([zi, si, zi, si], axis=1),
    ]
    return jnp.stack(tabs, axis=0)


def _work_items(counts, A, bm):
    starts = jnp.cumsum(counts) - counts
    ends = starts + counts
    nblk = A // bm
    bstart = jnp.arange(nblk, dtype=I32) * bm
    estart = starts[1:].astype(I32)
    bpos = jnp.arange(nblk, dtype=I32) + jnp.sum(estart[None, :] < bstart[:, None], axis=1).astype(I32)
    epos = jnp.arange(N_EXPERTS - 1, dtype=I32) + jnp.minimum(estart // bm + 1, nblk)
    bounds = jnp.zeros((nblk + N_EXPERTS - 1,), I32).at[bpos].set(bstart).at[epos].set(estart)
    nxt = jnp.concatenate([bounds[1:], jnp.array([A], I32)])
    blk = jnp.minimum(bounds // bm, nblk - 1)
    eid = jnp.minimum(jnp.sum(ends[None, :] <= bounds[:, None], axis=1), N_EXPERTS - 1).astype(I32)
    lo = bounds - blk * bm
    hi = nxt - blk * bm
    first = jnp.concatenate([jnp.ones((1,), I32), (blk[1:] != blk[:-1]).astype(I32)])
    newe = jnp.concatenate([jnp.ones((1,), I32), (eid[1:] != eid[:-1]).astype(I32)])
    return blk.astype(I32), eid, lo.astype(I32), hi.astype(I32), first, newe


def _pick(n, prefs):
    for p in prefs:
        if n % p == 0:
            return p
    return n


def kernel(x, norm_mix, w_in, conv_w, conv_b, conv_ln_g, conv_ln_b, fox_fb, w_out, norm_ffn,
           w_router_group, b_router_group, w_router_expert, b_router_expert, w_gate, w_up, w_down,
           norm_final):
    B, L, D = x.shape
    T = B * L
    depth = w_in.shape[0]
    k_sel = min(TOPK_MAX, L // 4)
    tabs = _rope_tables(L)
    fox_tq = _pick(L, (256, 128))
    bm = _pick(2 * T, (512, 256))
    td = _pick(T, (512, 256, 128))

    tk = _ATT_TK
    nk = L // tk
    fox_tk = min(_FOX_TK, fox_tq)
    segs_a = [("plain", 0, 2 * D_CONV, 0, 0, (), 1.0),
              ("plain", _O_QF, D_FOX, 1, 0, (), HEAD_DIM ** -0.5),
              ("plain", _O_KF, D_FOX, 1, D_FOX, (), 1.0)]
    outs_a = [(2 * D_CONV, BF16), (2 * D_FOX, BF16)]
    c_kd = D_DSA
    c_qi = c_kd + HEAD_DIM
    c_kk = c_qi + IDX_HEADS * IDX_DIM
    c_sm = c_kk + LANES
    segs_b = [("rope128", 0, D_DSA, 0, 0, (0, 1), 1.0),
              ("rope128", c_kd, HEAD_DIM, 1, 0, (2, 3), 1.0),
              ("rope64", c_qi, IDX_HEADS * IDX_DIM, 2, 0, (4, 5, 6), 1.0),
              ("rope64", c_kk, LANES, 3, 0, (4, 5, 6), 1.0),
              ("plain", c_sm, LANES, 4, 0, (), 1.0)]
    outs_b = [(D_DSA, BF16), (HEAD_DIM, BF16), (IDX_HEADS * IDX_DIM, BF16), (LANES, BF16), (LANES, F32)]

    xf = x.reshape(T, D)
    for l in range(depth):
        wl = w_in[l]
        w_a = wl[:, :_O_VF].astype(BF16)
        w_vf_t = wl[:, _O_VF:_O_FL].T.astype(BF16)
        w_ki = wl[:, _O_KI:_O_WI]
        pad = jnp.zeros((D, LANES - IDX_DIM - IDX_HEADS - FOX_HEADS), F32)
        w_b = jnp.concatenate([wl[:, _O_QD:_O_VD], wl[:, _O_QI:_O_KI], w_ki, w_ki,
                               w_ki, wl[:, _O_WI:_O_END], wl[:, _O_FL:_O_QD], pad], axis=1).astype(BF16)
        w_vd_t = wl[:, _O_VD:_O_QI].T.astype(BF16)
        w_wi_t = wl[:, _O_WI:_O_END].T.astype(BF16)
        fb_row = jnp.zeros((1, LANES), F32).at[0, _S_FL:_S_FL + FOX_HEADS].set(fox_fb[l])

        u, fqk, fvt = _proj(xf, norm_mix[l], w_a, tabs, segs_a, outs_a, L,
                            t_groups=[(w_vf_t, fox_tk, BF16)])
        qd, kd, qi, kdup, small, dvt, wit = _proj(xf, norm_mix[l], w_b, tabs, segs_b, outs_b, L,
                                                 t_groups=[(w_vd_t, tk, BF16), (w_wi_t, tk, F32)])

        c_tm = _fox_cumsum(small.reshape(B, L, LANES), fb_row)
        ya = _conv_module(u.reshape(B, L, -1), conv_w[l], conv_b[l], conv_ln_g[l], conv_ln_b[l])
        yb = _fox_attention(fqk.reshape(B, L, -1), fvt.reshape(B, L // fox_tk, D_FOX, fox_tk), c_tm,
                            tq=fox_tq)
        yc = _dsa_attention(qd.reshape(B, L, -1), kd.reshape(B, L, -1), qi.reshape(B, L, -1),
                            kdup.reshape(B, L, -1), dvt.reshape(B, nk, HEAD_DIM, tk), wit, k_sel)

        wr = jnp.concatenate([w_router_group[l], w_router_expert[l],
                              jnp.zeros((D, LANES - N_GROUPS - N_EXPERTS), F32)], axis=1)
        wr_hi = wr.astype(BF16)
        wr_lo = (wr - wr_hi.astype(F32)).astype(BF16)
        br = jnp.concatenate([b_router_group[l], b_router_expert[l],
                              jnp.zeros((LANES - N_GROUPS - N_EXPERTS,), F32)]).reshape(1, LANES)
        g_ffn = norm_ffn[l].reshape(1, D)
        x2, logits, hpk = _outproj(xf, ya.reshape(T, -1), yb.reshape(T, -1), yc.reshape(T, -1),
                                   w_out[l].astype(BF16), g_ffn, wr_hi, wr_lo, br)

        info, gates, cnt = _route(logits)
        counts = cnt[0, :N_EXPERTS].astype(I32)
        starts = jnp.cumsum(counts) - counts
        dest = starts[info[:, 0:2]] + info[:, 2:4]
        dest3 = dest.reshape(T // td, td, 2).transpose(0, 2, 1)

        xs = _dispatch(hpk, dest3, td)
        items = _work_items(counts, 2 * T, bm)
        ys = _expert_ffn(xs, w_gate, w_up, w_down, l, items, bm)
        xf = _combine(x2, dest3, gates, ys, norm_final.reshape(1, D), td, final=(l == depth - 1))
    return xf.reshape(B, L, D)
```

```python
import functools

import jax
import jax.numpy as jnp
import numpy as np
from jax import lax
from jax.experimental import pallas as pl
from jax.experimental.pallas import tpu as pltpu

F32 = jnp.float32
BF16 = jnp.bfloat16
I32 = jnp.int32

D_MODEL = 2048
HEAD_DIM = 128
D_CONV = 512
CONV_WIDTH = 31
FOX_HEADS = 6
DSA_HEADS = 6
D_FOX = FOX_HEADS * HEAD_DIM
D_DSA = DSA_HEADS * HEAD_DIM
IDX_HEADS = 16
IDX_DIM = 64
TOPK_MAX = 256
ROPE_THETA = 10000.0
CHUNK = 64
N_GROUPS = 4
EXPERTS_PER_GROUP = 8
N_EXPERTS = N_GROUPS * EXPERTS_PER_GROUP
D_EXPERT = 512
EPS = 1e-6

LANES = 128
NEG = -1e30
INT_MIN = -(2 ** 31)
VMEM_LIMIT = 56 * 1024 * 1024

_O_U = 0
_O_QF = _O_U + 2 * D_CONV
_O_KF = _O_QF + D_FOX
_O_VF = _O_KF + D_FOX
_O_FL = _O_VF + D_FOX
_O_QD = _O_FL + FOX_HEADS
_O_KD = _O_QD + D_DSA
_O_VD = _O_KD + HEAD_DIM
_O_QI = _O_VD + HEAD_DIM
_O_KI = _O_QI + IDX_HEADS * IDX_DIM
_O_WI = _O_KI + IDX_DIM
_O_END = _O_WI + IDX_HEADS

_S_KI = 0
_S_WI = IDX_DIM
_S_FL = IDX_DIM + IDX_HEADS


def _cparams(sem):
    return pltpu.CompilerParams(dimension_semantics=sem, vmem_limit_bytes=VMEM_LIMIT)


def _proj_body(x_ref, g_ref, w_ref, tab_ref, *rest, segs, n_t, n_out):
    wts, outs, outs_t = rest[:n_t], rest[n_t:n_t + n_out], rest[n_t + n_out:]
    x = x_ref[...]
    inv = lax.rsqrt(jnp.mean(x * x, axis=-1, keepdims=True) + EPS)
    h = (x * inv * g_ref[...]).astype(BF16)
    for (kind, w0, width, oi, o0, tabs, scale) in segs:
        z = jnp.dot(h, w_ref[:, w0:w0 + width], preferred_element_type=F32)
        for c in range(width // LANES):
            zc = z[:, c * LANES:(c + 1) * LANES]
            if kind == "plain":
                if scale != 1.0:
                    zc = zc * scale
            elif kind == "rope128":
                zc = zc * tab_ref[tabs[0]] + pltpu.roll(zc, 64, 1) * tab_ref[tabs[1]]
            else:
                zc = (zc * tab_ref[tabs[0]] + pltpu.roll(zc, 96, 1) * tab_ref[tabs[1]]
                      + pltpu.roll(zc, 32, 1) * tab_ref[tabs[2]])
            outs[oi][:, o0 + c * LANES:o0 + (c + 1) * LANES] = zc.astype(outs[oi].dtype)
    for wt_ref, o_ref in zip(wts, outs_t):
        zt = lax.dot_general(wt_ref[...], h, (((1,), (1,)), ((), ())), preferred_element_type=F32)
        tw = o_ref.shape[2]
        for c in range(o_ref.shape[0]):
            o_ref[c] = zt[:, c * tw:(c + 1) * tw].astype(o_ref.dtype)


def _proj(x2d, g, w, tabs, segs, out_defs, seq_len, t_groups=(), tm=512):
    T, D = x2d.shape
    tm = min(tm, seq_len)
    nt = T // tm
    nl = seq_len // tm
    ntab = tabs.shape[0]
    in_specs = [
        pl.BlockSpec((tm, D), lambda i: (i, 0)),
        pl.BlockSpec((1, D), lambda i: (0, 0)),
        pl.BlockSpec(w.shape, lambda i: (0, 0)),
        pl.BlockSpec((ntab, tm, LANES), lambda i: (0, i % nl, 0)),
    ]
    args = [x2d, g.reshape(1, D), w, tabs]
    out_shape = [jax.ShapeDtypeStruct((T, n), dt) for (n, dt) in out_defs]
    out_specs = [pl.BlockSpec((tm, n), lambda i: (i, 0)) for (n, dt) in out_defs]
    for (wt, tw, dt) in t_groups:
        in_specs.append(pl.BlockSpec(wt.shape, lambda i: (0, 0)))
        args.append(wt)
    for (wt, tw, dt) in t_groups:
        out_shape.append(jax.ShapeDtypeStruct((T // tw, wt.shape[0], tw), dt))
        out_specs.append(pl.BlockSpec((tm // tw, wt.shape[0], tw), lambda i: (i, 0, 0)))
    return pl.pallas_call(
        functools.partial(_proj_body, segs=tuple(segs), n_t=len(t_groups), n_out=len(out_defs)),
        grid=(nt,),
        in_specs=in_specs,
        out_specs=out_specs,
        out_shape=out_shape,
        compiler_params=_cparams(("parallel",)),
        name="proj",
    )(*args)


def _cumsum_body(fl_ref, fb_ref, o_ref):
    z = fl_ref[0] + fb_ref[...]
    x = jnp.minimum(z, 0.0) - jnp.log(1.0 + jnp.exp(-jnp.abs(z)))
    L = x.shape[0]
    row = lax.broadcasted_iota(I32, x.shape, 0)
    s = 1
    while s < L:
        x = x + jnp.where(row >= s, pltpu.roll(x, s, 0), 0.0)
        s *= 2
    o_ref[0] = x


def _fox_cumsum(small, fb_row):
    B, L, _ = small.shape
    return pl.pallas_call(
        _cumsum_body,
        grid=(B,),
        in_specs=[pl.BlockSpec((1, L, LANES), lambda b: (b, 0, 0)),
                  pl.BlockSpec((1, LANES), lambda b: (0, 0))],
        out_specs=pl.BlockSpec((1, L, LANES), lambda b: (b, 0, 0)),
        out_shape=jax.ShapeDtypeStruct((B, L, LANES), F32),
        compiler_params=_cparams(("parallel",)),
        name="fox_cumsum",
    )(small, fb_row)


_CONV_HALO = 32


def _conv_body(u_ref, w_ref, cb_ref, g_ref, b_ref, o_ref, a_scr, *, rc):
    L = u_ref.shape[1]
    a_scr[0:_CONV_HALO, :] = jnp.zeros((_CONV_HALO, D_CONV), F32)
    u1 = u_ref[0, :, 0:D_CONV].astype(F32)
    u2 = u_ref[0, :, D_CONV:2 * D_CONV].astype(F32)
    a_scr[_CONV_HALO:_CONV_HALO + L, :] = u1 * jax.nn.sigmoid(u2)
    win = rc + _CONV_HALO

    def chunk(r, carry):
        base = pl.multiple_of(r * rc, 8)
        wnd = a_scr[pl.ds(base, win), :]
        acc = jnp.zeros((rc, D_CONV), F32) + cb_ref[...]
        for j in range(CONV_WIDTH):
            off = _CONV_HALO - (CONV_WIDTH - 1) + j
            sh = pltpu.roll(wnd, win - off, 0)[0:rc]
            acc = acc + sh * w_ref[j:j + 1, :]
        mu = jnp.mean(acc, axis=-1, keepdims=True)
        d = acc - mu
        var = jnp.mean(d * d, axis=-1, keepdims=True)
        y = d * lax.rsqrt(var + EPS) * g_ref[...] + b_ref[...]
        o_ref[0, pl.ds(base, rc), :] = (y * jax.nn.sigmoid(y)).astype(o_ref.dtype)
        return carry

    lax.fori_loop(0, L // rc, chunk, 0)


def _conv_module(u, conv_w, conv_b, ln_g, ln_b, rc=256):
    B, L, _ = u.shape
    rc = min(rc, L)
    return pl.pallas_call(
        functools.partial(_conv_body, rc=rc),
        grid=(B,),
        in_specs=[pl.BlockSpec((1, L, 2 * D_CONV), lambda b: (b, 0, 0)),
                  pl.BlockSpec((CONV_WIDTH, D_CONV), lambda b: (0, 0)),
                  pl.BlockSpec((1, D_CONV), lambda b: (0, 0)),
                  pl.BlockSpec((1, D_CONV), lambda b: (0, 0)),
                  pl.BlockSpec((1, D_CONV), lambda b: (0, 0))],
        out_specs=pl.BlockSpec((1, L, D_CONV), lambda b: (b, 0, 0)),
        out_shape=jax.ShapeDtypeStruct((B, L, D_CONV), BF16),
        scratch_shapes=[pltpu.VMEM((L + _CONV_HALO, D_CONV), F32)],
        compiler_params=_cparams(("parallel",)),
        name="conv_module",
    )(u, conv_w, conv_b.reshape(1, -1), ln_g.reshape(1, -1), ln_b.reshape(1, -1))


_ATT_TK = 128
_FOX_TK = 128


def _fox_body(q_ref, k_ref, vt_ref, c_ref, o_ref, cb_scr, acc_scr, *, tq, tk):
    qi = pl.program_id(1)
    L = k_ref.shape[1]
    H = FOX_HEADS
    nsub = tq // tk

    @pl.when(qi == 0)
    def _():
        def fill(r, carry):
            off = pl.multiple_of(r * tk, tk)
            c = c_ref[0, pl.ds(off, tk), :]
            for h in range(H):
                cb_scr[h, pl.ds(off, tk), :] = jnp.broadcast_to(
                    c[:, _S_FL + h:_S_FL + h + 1], (tk, LANES))
            return carry

        lax.fori_loop(0, L // tk, fill, 0)

    q = [q_ref[0, :, h * HEAD_DIM:(h + 1) * HEAD_DIM] for h in range(H)]
    acc_scr[...] = jnp.zeros(acc_scr.shape, F32)

    def tile(j, carry, diag):
        ms, ls = carry
        off = pl.multiple_of(j * tk, tk)
        new_m, new_l, alphas, ps = [], [], [], []
        ss = [lax.dot_general(k_ref[0, pl.ds(off, tk), h * HEAD_DIM:(h + 1) * HEAD_DIM], q[h],
                              (((1,), (1,)), ((), ())), preferred_element_type=F32) for h in range(H)]
        for h in range(H):
            cb = cb_scr[h, pl.ds(off, tk), :]
            s = ss[h] - jnp.concatenate([cb] * (tq // LANES), axis=1)
            if diag:
                kk = off + lax.broadcasted_iota(I32, s.shape, 0)
                qq = qi * tq + lax.broadcasted_iota(I32, s.shape, 1)
                s = jnp.where(kk <= qq, s, NEG)
            m_new = jnp.maximum(ms[h], jnp.max(s, axis=0, keepdims=True))
            alpha = jnp.exp(ms[h] - m_new)
            p = jnp.exp(s - m_new)
            new_l.append(ls[h] * alpha + jnp.sum(p, axis=0, keepdims=True))
            new_m.append(m_new)
            alphas.append(alpha)
            ps.append(p.astype(BF16))
        pvs = [jnp.dot(vt_ref[0, j, h * HEAD_DIM:(h + 1) * HEAD_DIM, :], ps[h],
                       preferred_element_type=F32) for h in range(H)]
        for h in range(H):
            acc_scr[h] = acc_scr[h] * alphas[h] + pvs[h]
        return tuple(new_m), tuple(new_l)

    init = (tuple(jnp.full((1, tq), NEG, F32) for _ in range(H)),
            tuple(jnp.zeros((1, tq), F32) for _ in range(H)))
    carry = lax.fori_loop(0, qi * nsub, functools.partial(tile, diag=False), init)
    for d in range(nsub):
        carry = tile(qi * nsub + d, carry, True)
    ms, ls = carry
    for h in range(H):
        out = acc_scr[h] * (1.0 / ls[h])
        o_ref[0, :, h * HEAD_DIM:(h + 1) * HEAD_DIM] = out.T.astype(o_ref.dtype)


def _fox_attention(qk, vt, c_tm, tq=256):
    B, L, _ = qk.shape
    tq = min(tq, L)
    nq = L // tq
    nk, tk = vt.shape[1], vt.shape[3]
    return pl.pallas_call(
        functools.partial(_fox_body, tq=tq, tk=tk),
        grid=(B, nq),
        in_specs=[pl.BlockSpec((1, tq, D_FOX), lambda b, i: (b, i, 0)),
                  pl.BlockSpec((1, L, D_FOX), lambda b, i: (b, 0, 1)),
                  pl.BlockSpec((1, nk, D_FOX, tk), lambda b, i: (b, 0, 0, 0)),
                  pl.BlockSpec((1, L, LANES), lambda b, i: (b, 0, 0))],
        out_specs=pl.BlockSpec((1, tq, D_FOX), lambda b, i: (b, i, 0)),
        out_shape=jax.ShapeDtypeStruct((B, L, D_FOX), BF16),
        scratch_shapes=[pltpu.VMEM((FOX_HEADS, L, LANES), F32),
                        pltpu.VMEM((FOX_HEADS, HEAD_DIM, tq), F32)],
        compiler_params=_cparams(("arbitrary", "arbitrary")),
        name="fox_attention",
    )(qk, qk, vt, c_tm)


def _dsa_body(qd_ref, qi_ref, wt_ref, k_ref, kdup_ref, vt_ref, o_ref,
              key_scr, rhs_scr, acc_scr, thr_scr, *, tq, k_sel, nvs):
    qb = pl.program_id(1)
    H = DSA_HEADS
    half = IDX_DIM

    lane = lax.broadcasted_iota(I32, (tq, LANES), 1)
    for p in range(IDX_HEADS // 2):
        qp = qi_ref[0, :, p * LANES:(p + 1) * LANES].astype(F32)
        rhs_scr[(2 * p) * tq:(2 * p + 1) * tq, :] = jnp.where(lane < half, qp, 0.0).astype(BF16)
        rhs_scr[(2 * p + 1) * tq:(2 * p + 2) * tq, :] = jnp.where(lane >= half, qp, 0.0).astype(BF16)
    wq = wt_ref[0] * ((IDX_DIM ** -0.5) * (IDX_HEADS ** -0.5))

    key_scr[...] = jnp.full(key_scr.shape, INT_MIN, I32)
    rowk = lax.broadcasted_iota(I32, (tq, tq), 0)
    colq = lax.broadcasted_iota(I32, (tq, tq), 1)
    admissible_diag = (rowk // CHUNK) <= (colq // CHUNK)

    npair = (qb + 2) // 2

    def score_pair(jj, carry):
        for g in range(2):
            j = 2 * jj + g
            off = pl.multiple_of(j * tq, tq)
            kd = kdup_ref[0, pl.ds(off, tq), :]
            s = jnp.zeros((tq, tq), F32)
            for p in range(IDX_HEADS // 2):
                r = lax.dot_general(kd, rhs_scr[(2 * p) * tq:(2 * p + 2) * tq, :],
                                    (((1,), (1,)), ((), ())), preferred_element_type=F32)
                s = s + jnp.maximum(r[:, :tq], 0.0) * wq[2 * p:2 * p + 1, :]
                s = s + jnp.maximum(r[:, tq:], 0.0) * wq[2 * p + 1:2 * p + 2, :]
            bits = pltpu.bitcast(s, I32)
            key = jnp.where(bits < 0, bits ^ jnp.int32(0x7FFFFFFF), bits)
            adm = jnp.logical_or(j < qb, jnp.logical_and(j == qb, admissible_diag))
            key_scr[j] = jnp.where(adm, key, INT_MIN)
        return carry

    lax.fori_loop(0, npair, score_pair, 0)

    def select(nv):
        def bit_step(i, t):
            cand = t + (jnp.int32(1) << (31 - i))
            cnt = jnp.sum(jnp.where(key_scr[0:nv] >= cand[None], 1.0, 0.0), axis=0)
            cnt = jnp.sum(cnt, axis=0, keepdims=True)
            return jnp.where(cnt >= float(k_sel), cand, t)

        t = lax.fori_loop(0, 32, bit_step, jnp.full((1, tq), INT_MIN, I32))
        t = jnp.maximum(t, INT_MIN + 1)
        thr_scr[...] = jnp.broadcast_to(t, thr_scr.shape)

    for idx, nv in enumerate(nvs):
        lo = nvs[idx - 1] if idx else 0

        @pl.when(jnp.logical_and(qb + 1 > lo, qb + 1 <= nv))
        def _(nv=nv):
            select(nv)

    thr = thr_scr[0:1, :]
    qs = jnp.concatenate([qd_ref[0, :, h * HEAD_DIM:(h + 1) * HEAD_DIM] for h in range(H)], axis=0)
    acc_scr[...] = jnp.zeros(acc_scr.shape, F32)

    def attn_pair(jj, carry):
        m, l = carry
        off = pl.multiple_of(jj * (2 * tq), 2 * tq)
        k = k_ref[0, pl.ds(off, 2 * tq), :]
        s = lax.dot_general(k, qs, (((1,), (1,)), ((), ())), preferred_element_type=F32)
        sel = key_scr[pl.ds(2 * jj, 2)].reshape(2 * tq, tq) >= thr
        s = jnp.concatenate([jnp.where(sel, s[:, h * tq:(h + 1) * tq], NEG) for h in range(H)], axis=1)
        m_new = jnp.maximum(m, jnp.max(s, axis=0, keepdims=True))
        alpha = jnp.exp(m - m_new)
        p = jnp.exp(s - m_new)
        l = l * alpha + jnp.sum(p, axis=0, keepdims=True)
        vt = jnp.concatenate([vt_ref[0, 2 * jj], vt_ref[0, 2 * jj + 1]], axis=1)
        acc_scr[...] = acc_scr[...] * alpha + jnp.dot(vt, p.astype(BF16), preferred_element_type=F32)
        return m_new, l

    init = (jnp.full((1, H * tq), NEG, F32), jnp.zeros((1, H * tq), F32))
    m, l = lax.fori_loop(0, npair, attn_pair, init)
    out = acc_scr[...] * (1.0 / l)
    for h in range(H):
        o_ref[0, :, h * HEAD_DIM:(h + 1) * HEAD_DIM] = out[:, h * tq:(h + 1) * tq].T.astype(o_ref.dtype)


def _dsa_attention(qd, kd, qi, kdup, vt, wit, k_sel):
    B, L, _ = qd.shape
    tq = _ATT_TK
    nq = L // tq
    nvs = tuple(sorted({-(-nq * f // 4) for f in (1, 2, 3, 4)}))
    return pl.pallas_call(
        functools.partial(_dsa_body, tq=tq, k_sel=k_sel, nvs=nvs),
        grid=(B, nq),
        in_specs=[pl.BlockSpec((1, tq, D_DSA), lambda b, i: (b, i, 0)),
                  pl.BlockSpec((1, tq, IDX_HEADS * IDX_DIM), lambda b, i: (b, i, 0)),
                  pl.BlockSpec((1, IDX_HEADS, tq), lambda b, i: (b * nq + i, 0, 0)),
                  pl.BlockSpec((1, L, HEAD_DIM), lambda b, i: (b, 0, 0)),
                  pl.BlockSpec((1, L, LANES), lambda b, i: (b, 0, 0)),
                  pl.BlockSpec((1, nq, HEAD_DIM, tq), lambda b, i: (b, 0, 0, 0))],
        out_specs=pl.BlockSpec((1, tq, D_DSA), lambda b, i: (b, i, 0)),
        out_shape=jax.ShapeDtypeStruct((B, L, D_DSA), BF16),
        scratch_shapes=[pltpu.VMEM((nq, tq, tq), I32),
                        pltpu.VMEM((IDX_HEADS * tq, LANES), BF16),
                        pltpu.VMEM((HEAD_DIM, DSA_HEADS * tq), F32),
                        pltpu.VMEM((8, tq), I32)],
        compiler_params=_cparams(("arbitrary", "arbitrary")),
        name="dsa_attention",
    )(qd, qi, wit, kd, kdup, vt)


U32 = jnp.uint32
_HALF = D_MODEL // 2


def _pack_halves(a, b):
    ah = pltpu.bitcast(a.astype(BF16).astype(F32), U32)
    bh = pltpu.bitcast(b.astype(BF16).astype(F32), U32)
    return ah | (bh >> 16)


def _unpack_halves(u):
    a = pltpu.bitcast(u & jnp.uint32(0xFFFF0000), F32)
    b = pltpu.bitcast(u << 16, F32)
    return a, b


def _outproj_body(x_ref, ya_ref, yb_ref, yc_ref, w_ref, g_ref, wrh_ref, wrl_ref, br_ref,
                  x2_ref, lg_ref, hp_ref):
    acc = x_ref[...]
    acc = acc + jnp.dot(ya_ref[...], w_ref[0:D_CONV, :], preferred_element_type=F32)
    acc = acc + jnp.dot(yb_ref[...], w_ref[D_CONV:D_CONV + D_FOX, :], preferred_element_type=F32)
    acc = acc + jnp.dot(yc_ref[...], w_ref[D_CONV + D_FOX:, :], preferred_element_type=F32)
    x2_ref[...] = acc
    h = acc * lax.rsqrt(jnp.mean(acc * acc, axis=-1, keepdims=True) + EPS) * g_ref[...]
    hh = h.astype(BF16)
    hl = (h - hh.astype(F32)).astype(BF16)
    lg = jnp.dot(hh, wrh_ref[...], preferred_element_type=F32)
    lg = lg + jnp.dot(hh, wrl_ref[...], preferred_element_type=F32)
    lg = lg + jnp.dot(hl, wrh_ref[...], preferred_element_type=F32)
    lg_ref[...] = lg + br_ref[...]
    hp_ref[...] = _pack_halves(h[:, :_HALF], h[:, _HALF:])


def _outproj(x2d, ya, yb, yc, w_out, g_ffn, wr_hi, wr_lo, br, tm=512):
    T, D = x2d.shape
    tm = min(tm, T)
    row = lambda n: pl.BlockSpec((tm, n), lambda i: (i, 0))
    const = lambda a: pl.BlockSpec(a.shape, lambda i: (0, 0))
    return pl.pallas_call(
        _outproj_body,
        grid=(T // tm,),
        in_specs=[row(D), row(D_CONV), row(D_FOX), row(D_DSA), const(w_out), const(g_ffn),
                  const(wr_hi), const(wr_lo), const(br)],
        out_specs=[row(D), row(LANES), row(_HALF)],
        out_shape=[jax.ShapeDtypeStruct((T, D), F32), jax.ShapeDtypeStruct((T, LANES), F32),
                   jax.ShapeDtypeStruct((T, _HALF), U32)],
        compiler_params=_cparams(("parallel",)),
        name="outproj",
    )(x2d, ya, yb, yc, w_out, g_ffn, wr_hi, wr_lo, br)


def _route_body(lg_ref, info_ref, gate_ref, cnt_ref):
    i = pl.program_id(0)
    lg = lg_ref[...]
    tm = lg.shape[0]
    lane = lax.broadcasted_iota(I32, lg.shape, 1)
    ninf = -jnp.inf

    lane_f = lane.astype(F32)

    def first_lane(mask):
        return jnp.min(jnp.where(mask, lane_f, float(LANES)), axis=-1, keepdims=True).astype(I32)

    gl = jnp.where(lane < N_GROUPS, lg, ninf)
    gexp = jnp.exp(gl - jnp.max(gl, axis=-1, keepdims=True))
    gprob = gexp / jnp.sum(gexp, axis=-1, keepdims=True)
    p_g = jnp.max(gprob, axis=-1, keepdims=True)
    g_idx = first_lane(gprob == p_g)

    e_lo = N_GROUPS + EXPERTS_PER_GROUP * g_idx
    emask = jnp.logical_and(lane >= e_lo, lane < e_lo + EXPERTS_PER_GROUP)
    el = jnp.where(emask, lg, ninf)
    eexp = jnp.exp(el - jnp.max(el, axis=-1, keepdims=True))
    eprob = jnp.where(emask, eexp / jnp.sum(eexp, axis=-1, keepdims=True), -1.0)
    p1 = jnp.max(eprob, axis=-1, keepdims=True)
    l1 = first_lane(eprob == p1)
    eprob2 = jnp.where(lane == l1, -1.0, eprob)
    p2 = jnp.max(eprob2, axis=-1, keepdims=True)
    l2 = first_lane(eprob2 == p2)
    den = p1 + p2
    gate1 = p_g * p1 / den
    gate2 = p_g * p2 / den
    eid1 = l1 - N_GROUPS
    eid2 = l2 - N_GROUPS

    @pl.when(i == 0)
    def _():
        cnt_ref[...] = jnp.zeros(cnt_ref.shape, F32)

    hot1 = lane == eid1
    hot2 = lane == eid2
    onehot = jnp.where(jnp.logical_or(hot1, hot2), 1.0, 0.0)
    r = lax.broadcasted_iota(I32, (tm, tm), 0)
    c = lax.broadcasted_iota(I32, (tm, tm), 1)
    tri = jnp.where(c < r, 1.0, 0.0).astype(BF16)
    before = jnp.dot(tri, onehot.astype(BF16), preferred_element_type=F32) + cnt_ref[0:1, :]
    rank1 = jnp.sum(jnp.where(hot1, before, 0.0), axis=-1, keepdims=True).astype(I32)
    rank2 = jnp.sum(jnp.where(hot2, before, 0.0), axis=-1, keepdims=True).astype(I32)
    cnt_ref[...] = cnt_ref[...] + jnp.sum(onehot, axis=0, keepdims=True)

    info_ref[...] = jnp.where(lane == 0, eid1, jnp.where(lane == 1, eid2,
                              jnp.where(lane == 2, rank1, jnp.where(lane == 3, rank2, 0))))
    gate_ref[...] = jnp.where(lane == 0, gate1, jnp.where(lane == 1, gate2, 0.0))


def _route(logits, tm=512):
    T = logits.shape[0]
    tm = min(tm, T)
    row = pl.BlockSpec((tm, LANES), lambda i: (i, 0))
    return pl.pallas_call(
        _route_body,
        grid=(T // tm,),
        in_specs=[row],
        out_specs=[row, row, pl.BlockSpec((8, LANES), lambda i: (0, 0))],
        out_shape=[jax.ShapeDtypeStruct((T, LANES), I32), jax.ShapeDtypeStruct((T, LANES), F32),
                   jax.ShapeDtypeStruct((8, LANES), F32)],
        compiler_params=_cparams(("arbitrary",)),
        name="route",
    )(logits)


def _row_copy(src, s, dst, d, sem):
    return pltpu.make_async_copy(src.at[pl.ds(s, 1)], dst.at[pl.ds(d, 1)], sem)


def _dispatch_body(dest_ref, x_ref, xs_hbm, sem, *, td):
    def issue(t, carry):
        _row_copy(x_ref, t, xs_hbm, dest_ref[0, 0, t], sem).start()
        _row_copy(x_ref, t, xs_hbm, dest_ref[0, 1, t], sem).start()
        return carry

    lax.fori_loop(0, td, issue, 0, unroll=8)
    pltpu.make_async_copy(xs_hbm.at[pl.ds(0, 2 * td)], xs_hbm.at[pl.ds(0, 2 * td)], sem).wait()


def _dispatch(x2, dest3, td):
    T, D = x2.shape
    return pl.pallas_call(
        functools.partial(_dispatch_body, td=td),
        grid=(T // td,),
        in_specs=[pl.BlockSpec((1, 2, td), lambda i: (i, 0, 0), memory_space=pltpu.SMEM),
                  pl.BlockSpec((td, D), lambda i: (i, 0))],
        out_specs=pl.BlockSpec(memory_space=pl.ANY),
        out_shape=jax.ShapeDtypeStruct((2 * T, D), x2.dtype),
        scratch_shapes=[pltpu.SemaphoreType.DMA(())],
        compiler_params=_cparams(("arbitrary",)),
        name="dispatch",
    )(dest3, x2)


def _expert_body(blk_ref, e_ref, lo_ref, hi_ref, first_ref, newe_ref, xs_ref, wg_ref, wu_ref, wd_ref,
                 o_ref, wg_scr, wu_scr, wd_scr):
    t = pl.program_id(0)
    lo = lo_ref[t]
    hi = hi_ref[t]

    @pl.when(newe_ref[t] == 1)
    def _():
        wg_scr[...] = wg_ref[0].astype(BF16)
        wu_scr[...] = wu_ref[0].astype(BF16)
        wd_scr[...] = wd_ref[0].astype(BF16)

    @pl.when(first_ref[t] == 1)
    def _():
        o_ref[...] = jnp.zeros(o_ref.shape, U32)

    @pl.when(hi > lo)
    def _():
        ha, hb = _unpack_halves(xs_ref[...])
        h = jnp.concatenate([ha.astype(BF16), hb.astype(BF16)], axis=1)
        g = jnp.dot(h, wg_scr[...], preferred_element_type=F32)
        u = jnp.dot(h, wu_scr[...], preferred_element_type=F32)
        hdn = (g * jax.nn.sigmoid(g) * u).astype(BF16)
        y = jnp.dot(hdn, wd_scr[...], preferred_element_type=F32)
        row = lax.broadcasted_iota(I32, (y.shape[0], 1), 0)
        keep = jnp.logical_and(row >= lo, row < hi)
        o_ref[...] = jnp.where(keep, _pack_halves(y[:, :_HALF], y[:, _HALF:]), o_ref[...])


def _expert_ffn(xs, wg, wu, wd, layer, items, bm):
    A, DH = xs.shape
    D = 2 * DH
    blk, eid, lo, hi, first, newe = items
    ni = blk.shape[0]
    grid_spec = pltpu.PrefetchScalarGridSpec(
        num_scalar_prefetch=6,
        grid=(ni,),
        in_specs=[pl.BlockSpec((bm, DH), lambda t, b, e, lo, hi, f, n: (b[t], 0)),
                  pl.BlockSpec((None, 1, D, D_EXPERT), lambda t, b, e, lo, hi, f, n: (layer, e[t], 0, 0)),
                  pl.BlockSpec((None, 1, D, D_EXPERT), lambda t, b, e, lo, hi, f, n: (layer, e[t], 0, 0)),
                  pl.BlockSpec((None, 1, D_EXPERT, D), lambda t, b, e, lo, hi, f, n: (layer, e[t], 0, 0))],
        out_specs=pl.BlockSpec((bm, DH), lambda t, b, e, lo, hi, f, n: (b[t], 0)),
        scratch_shapes=[pltpu.VMEM((D, D_EXPERT), BF16), pltpu.VMEM((D, D_EXPERT), BF16),
                        pltpu.VMEM((D_EXPERT, D), BF16)],
    )
    return pl.pallas_call(
        _expert_body,
        grid_spec=grid_spec,
        out_shape=jax.ShapeDtypeStruct((A, DH), U32),
        compiler_params=_cparams(("arbitrary",)),
        name="expert_ffn",
    )(blk, eid, lo, hi, first, newe, xs, wg, wu, wd)


def _combine_body(dest_ref, x_ref, gate_ref, gf_ref, ys_hbm, o_ref, buf, sem, *, tc, final):
    def issue(t, carry):
        pltpu.make_async_copy(ys_hbm.at[pl.ds(dest_ref[0, 0, t], 1)], buf.at[0, pl.ds(t, 1)], sem).start()
        pltpu.make_async_copy(ys_hbm.at[pl.ds(dest_ref[0, 1, t], 1)], buf.at[1, pl.ds(t, 1)], sem).start()
        return carry

    lax.fori_loop(0, tc, issue, 0, unroll=8)
    pltpu.make_async_copy(buf, buf, sem).wait()
    g = gate_ref[...]
    y0a, y0b = _unpack_halves(buf[0])
    y1a, y1b = _unpack_halves(buf[1])
    g0 = g[:, 0:1]
    g1 = g[:, 1:2]
    out = x_ref[...] + jnp.concatenate([g0 * y0a + g1 * y1a, g0 * y0b + g1 * y1b], axis=1)
    if final:
        out = out * lax.rsqrt(jnp.mean(out * out, axis=-1, keepdims=True) + EPS) * gf_ref[...]
    o_ref[...] = out


def _combine(x2, dest3, gates, ys, g_final, tc, final):
    T, D = x2.shape
    return pl.pallas_call(
        functools.partial(_combine_body, tc=tc, final=final),
        grid=(T // tc,),
        in_specs=[pl.BlockSpec((1, 2, tc), lambda i: (i, 0, 0), memory_space=pltpu.SMEM),
                  pl.BlockSpec((tc, D), lambda i: (i, 0)),
                  pl.BlockSpec((tc, LANES), lambda i: (i, 0)),
                  pl.BlockSpec((1, D), lambda i: (0, 0)),
                  pl.BlockSpec(memory_space=pl.ANY)],
        out_specs=pl.BlockSpec((tc, D), lambda i: (i, 0)),
        out_shape=jax.ShapeDtypeStruct((T, D), F32),
        scratch_shapes=[pltpu.VMEM((2, tc, D // 2), U32), pltpu.SemaphoreType.DMA(())],
        compiler_params=_cparams(("arbitrary",)),
        name="combine",
    )(dest3, x2, gates, g_final, ys)


def _rope_tables(L):
    pos = jnp.arange(L, dtype=F32)

    def cs(dim):
        inv = ROPE_THETA ** (-jnp.arange(0, dim, 2, dtype=F32) / dim)
        ang = pos[:, None] * inv[None, :]
        return jnp.cos(ang), jnp.sin(ang)

    ch, sh = cs(HEAD_DIM)
    ci, si = cs(IDX_DIM)
    zi = jnp.zeros_like(si)
    scale = HEAD_DIM ** -0.5
    cos128 = jnp.concatenate([ch, ch], axis=1)
    sin128 = jnp.concatenate([-sh, sh], axis=1)
    tabs = [
        cos128 * scale, sin128 * scale,
        cos128, sin128,
        jnp.concatenate([ci, ci, ci, ci], axis=1),
        jnp.concatenate([-si, zi, -si, zi], axis=1),
        jnp.concatenate([zi, si, zi, si], axis=1),
    ]
    return jnp.stack(tabs, axis=0)


def _work_items(counts, A, bm):
    starts = jnp.cumsum(counts) - counts
    ends = starts + counts
    nblk = A // bm
    bstart = jnp.arange(nblk, dtype=I32) * bm
    estart = starts[1:].astype(I32)
    bpos = jnp.arange(nblk, dtype=I32) + jnp.sum(estart[None, :] < bstart[:, None], axis=1).astype(I32)
    epos = jnp.arange(N_EXPERTS - 1, dtype=I32) + jnp.minimum(estart // bm + 1, nblk)
    bounds = jnp.zeros((nblk + N_EXPERTS - 1,), I32).at[bpos].set(bstart).at[epos].set(estart)
    nxt = jnp.concatenate([bounds[1:], jnp.array([A], I32)])
    blk = jnp.minimum(bounds // bm, nblk - 1)
    eid = jnp.minimum(jnp.sum(ends[None, :] <= bounds[:, None], axis=1), N_EXPERTS - 1).astype(I32)
    lo = bounds - blk * bm
    hi = nxt - blk * bm
    first = jnp.concatenate([jnp.ones((1,), I32), (blk[1:] != blk[:-1]).astype(I32)])
    newe = jnp.concatenate([jnp.ones((1,), I32), (eid[1:] != eid[:-1]).astype(I32)])
    return blk.astype(I32), eid, lo.astype(I32), hi.astype(I32), first, newe


def _pick(n, prefs):
    for p in prefs:
        if n % p == 0:
            return p
    return n


def kernel(x, norm_mix, w_in, conv_w, conv_b, conv_ln_g, conv_ln_b, fox_fb, w_out, norm_ffn,
           w_router_group, b_router_group, w_router_expert, b_router_expert, w_gate, w_up, w_down,
           norm_final):
    B, L, D = x.shape
    T = B * L
    depth = w_in.shape[0]
    k_sel = min(TOPK_MAX, L // 4)
    tabs = _rope_tables(L)
    fox_tq = _pick(L, (256, 128))
    bm = _pick(2 * T, (512, 256))
    td = _pick(T, (512, 256, 128))

    tk = _ATT_TK
    nk = L // tk
    fox_tk = min(_FOX_TK, fox_tq)
    segs_a = [("plain", 0, 2 * D_CONV, 0, 0, (), 1.0),
              ("plain", _O_QF, D_FOX, 1, 0, (), HEAD_DIM ** -0.5),
              ("plain", _O_KF, D_FOX, 1, D_FOX, (), 1.0)]
    outs_a = [(2 * D_CONV, BF16), (2 * D_FOX, BF16)]
    c_kd = D_DSA
    c_qi = c_kd + HEAD_DIM
    c_kk = c_qi + IDX_HEADS * IDX_DIM
    c_sm = c_kk + LANES
    segs_b = [("rope128", 0, D_DSA, 0, 0, (0, 1), 1.0),
              ("rope128", c_kd, HEAD_DIM, 1, 0, (2, 3), 1.0),
              ("rope64", c_qi, IDX_HEADS * IDX_DIM, 2, 0, (4, 5, 6), 1.0),
              ("rope64", c_kk, LANES, 3, 0, (4, 5, 6), 1.0),
              ("plain", c_sm, LANES, 4, 0, (), 1.0)]
    outs_b = [(D_DSA, BF16), (HEAD_DIM, BF16), (IDX_HEADS * IDX_DIM, BF16), (LANES, BF16), (LANES, F32)]

    xf = x.reshape(T, D)
    for l in range(depth):
        wl = w_in[l]
        w_a = wl[:, :_O_VF].astype(BF16)
        w_vf_t = wl[:, _O_VF:_O_FL].T.astype(BF16)
        w_ki = wl[:, _O_KI:_O_WI]
        pad = jnp.zeros((D, LANES - IDX_DIM - IDX_HEADS - FOX_HEADS), F32)
        w_b = jnp.concatenate([wl[:, _O_QD:_O_VD], wl[:, _O_QI:_O_KI], w_ki, w_ki,
                               w_ki, wl[:, _O_WI:_O_END], wl[:, _O_FL:_O_QD], pad], axis=1).astype(BF16)
        w_vd_t = wl[:, _O_VD:_O_QI].T.astype(BF16)
        w_wi_t = wl[:, _O_WI:_O_END].T.astype(BF16)
        fb_row = jnp.zeros((1, LANES), F32).at[0, _S_FL:_S_FL + FOX_HEADS].set(fox_fb[l])

        u, fqk, fvt = _proj(xf, norm_mix[l], w_a, tabs, segs_a, outs_a, L,
                            t_groups=[(w_vf_t, fox_tk, BF16)])
        qd, kd, qi, kdup, small, dvt, wit = _proj(xf, norm_mix[l], w_b, tabs, segs_b, outs_b, L,
                                                 t_groups=[(w_vd_t, tk, BF16), (w_wi_t, tk, F32)])

        c_tm = _fox_cumsum(small.reshape(B, L, LANES), fb_row)
        ya = _conv_module(u.reshape(B, L, -1), conv_w[l], conv_b[l], conv_ln_g[l], conv_ln_b[l])
        yb = _fox_attention(fqk.reshape(B, L, -1), fvt.reshape(B, L // fox_tk, D_FOX, fox_tk), c_tm,
                            tq=fox_tq)
        yc = _dsa_attention(qd.reshape(B, L, -1), kd.reshape(B, L, -1), qi.reshape(B, L, -1),
                            kdup.reshape(B, L, -1), dvt.reshape(B, nk, HEAD_DIM, tk), wit, k_sel)

        wr = jnp.concatenate([w_router_group[l], w_router_expert[l],
                              jnp.zeros((D, LANES - N_GROUPS - N_EXPERTS), F32)], axis=1)
        wr_hi = wr.astype(BF16)
        wr_lo = (wr - wr_hi.astype(F32)).astype(BF16)
        br = jnp.concatenate([b_router_group[l], b_router_expert[l],
                              jnp.zeros((LANES - N_GROUPS - N_EXPERTS,), F32)]).reshape(1, LANES)
        g_ffn = norm_ffn[l].reshape(1, D)
        x2, logits, hpk = _outproj(xf, ya.reshape(T, -1), yb.reshape(T, -1), yc.reshape(T, -1),
                                   w_out[l].astype(BF16), g_ffn, wr_hi, wr_lo, br)

        info, gates, cnt = _route(logits)
        counts = cnt[0, :N_EXPERTS].astype(I32)
        starts = jnp.cumsum(counts) - counts
        dest = starts[info[:, 0:2]] + info[:, 2:4]
        dest3 = dest.reshape(T // td, td, 2).transpose(0, 2, 1)

        xs = _dispatch(hpk, dest3, td)
        items = _work_items(counts, 2 * T, bm)
        ys = _expert_ffn(xs, w_gate, w_up, w_down, l, items, bm)
        xf = _combine(x2, dest3, gates, ys, norm_final.reshape(1, D), td, final=(l == depth - 1))
    return xf.reshape(B, L, D)
```
